```python
import math
import jax, jax.numpy as jnp
from jax import lax
import numpy as np

D_MODEL = 1024
BATCH = 8
SEQ = 2048
DEPTH = 4

D_RNN = D_MODEL
N_LRU_BLOCKS = 8
LRU_BLOCK = D_RNN // N_LRU_BLOCKS
RNN_CONV_WIDTH = 4
LRU_C = 8.0
HEAD_DIM = 128
HEADS_PER_GROUP = 4
DILATED_GROUPS = ((128, 1), (512, 4), (2048, 16))
N_GROUPS = len(DILATED_GROUPS)
N_ATT_HEADS = N_GROUPS * HEADS_PER_GROUP
ATT_WIDTH = N_ATT_HEADS * HEAD_DIM
ATT_OUT_WIDTH = HEADS_PER_GROUP * HEAD_DIM
BLOCK = 128
ROPE_THETA = 10000.0
D_FF = 3 * D_MODEL
FFN_CONV_WIDTH = 3
EPS = 1e-6
NEG_INF = -1e30
IN_SPLITS = (D_RNN, D_RNN, ATT_WIDTH, ATT_WIDTH, ATT_WIDTH, D_MODEL, D_MODEL)
N_IN = sum(IN_SPLITS)

kernel_name = "hybrid_rglru_dilated_attn_convffn"


def rms_norm(x, g):
    x32 = x.astype(jnp.float32)
    y = x32 * lax.rsqrt(jnp.mean(x32 * x32, axis=-1, keepdims=True) + EPS)
    return (y * g.astype(jnp.float32)).astype(x.dtype)


def causal_depthwise_conv(x, w, b):
    width = w.shape[0]
    s = x.shape[1]
    xp = jnp.pad(x, ((0, 0), (width - 1, 0), (0, 0)))
    out = b
    for i in range(width):
        out = out + xp[:, i:i + s] * w[i]
    return out


def block_diag_linear(x, w, b):
    bsz, s, c = x.shape
    nb = w.shape[0]
    y = jnp.einsum('bsni,nij->bsnj', x.reshape(bsz, s, nb, c // nb), w)
    return y.reshape(bsz, s, c) + b


def rg_lru(x, wa, ba, wx, bx, lam):
    r = jax.nn.sigmoid(block_diag_linear(x, wa, ba).astype(jnp.float32))
    i = jax.nn.sigmoid(block_diag_linear(x, wx, bx).astype(jnp.float32))
    log_a = -LRU_C * r * jax.nn.softplus(-lam.astype(jnp.float32))
    a = jnp.exp(log_a)
    mult = jnp.sqrt(-jnp.expm1(2.0 * log_a))
    u = mult * (i * x.astype(jnp.float32))

    def combine(left, right):
        a_l, b_l = left
        a_r, b_r = right
        return a_l * a_r, a_r * b_l + b_r

    _, h = lax.associative_scan(combine, (a, u), axis=1)
    return h.astype(x.dtype)


def rotary(x, positions):
    half = x.shape[-1] // 2
    inv_freq = ROPE_THETA ** (-jnp.arange(half, dtype=jnp.float32) / half)
    ang = positions.astype(jnp.float32)[..., None] * inv_freq
    cos = jnp.cos(ang)[:, :, None, :]
    sin = jnp.sin(ang)[:, :, None, :]
    x32 = x.astype(jnp.float32)
    x1, x2 = x32[..., :half], x32[..., half:]
    return jnp.concatenate([x1 * cos - x2 * sin, x2 * cos + x1 * sin], axis=-1).astype(x.dtype)


def dilated_window_attention(q, k, v, dilation, span):
    bsz, s, h, dh = q.shape
    n = s // dilation
    nb = -(-n // BLOCK)
    n_pad = nb * BLOCK

    def to_blocks(t):
        t = t.reshape(bsz, n, dilation, h, dh).transpose(0, 2, 1, 3, 4)
        t = jnp.pad(t, ((0, 0), (0, 0), (0, n_pad - n), (0, 0), (0, 0)))
        return t.reshape(bsz, dilation, nb, BLOCK, h, dh)

    def with_prev(t):
        prev = jnp.pad(t, ((0, 0), (0, 0), (1, 0), (0, 0), (0, 0), (0, 0)))[:, :, :-1]
        return jnp.concatenate([prev, t], axis=3)

    qb = to_blocks(q)
    kw = with_prev(to_blocks(k))
    vw = with_prev(to_blocks(v))
    scores = jnp.einsum('bgnqhd,bgnkhd->bgnhqk', qb, kw).astype(jnp.float32)
    qi = jnp.arange(BLOCK)[:, None]
    kj = jnp.arange(2 * BLOCK)[None, :]
    dist = qi + BLOCK - kj
    blk = jnp.arange(nb)[:, None, None]
    valid = (dist >= 0)[None] & (dist <= span)[None] & ((blk > 0) | (kj >= BLOCK)[None])
    scores = jnp.where(valid[None, None, :, None], scores, NEG_INF)
    m = jnp.max(scores, axis=-1, keepdims=True)
    p = jnp.exp(scores - m)
    den = jnp.sum(p, axis=-1, keepdims=True)
    out = jnp.einsum('bgnhqk,bgnkhd->bgnqhd', (p / den).astype(v.dtype), vw)
    lse = (m + jnp.log(den))[..., 0].transpose(0, 1, 2, 4, 3)
    out = out.reshape(bsz, dilation, n_pad, h, dh)[:, :, :n].transpose(0, 2, 1, 3, 4).reshape(bsz, s, h, dh)
    lse = lse.reshape(bsz, dilation, n_pad, h)[:, :, :n].transpose(0, 2, 1, 3).reshape(bsz, s, h)
    return out, lse


def dilated_attention_mixer(q, k, v, positions, q_g, k_g):
    bsz, s, _ = q.shape
    q = q.reshape(bsz, s, N_ATT_HEADS, HEAD_DIM)
    k = k.reshape(bsz, s, N_ATT_HEADS, HEAD_DIM)
    v = v.reshape(bsz, s, N_ATT_HEADS, HEAD_DIM)
    q = rotary(rms_norm(q, q_g), positions) * (HEAD_DIM ** -0.5)
    k = rotary(rms_norm(k, k_g), positions)
    outs, lses = [], []
    for gi, (window, dilation) in enumerate(DILATED_GROUPS):
        hs = slice(gi * HEADS_PER_GROUP, (gi + 1) * HEADS_PER_GROUP)
        o, l = dilated_window_attention(q[:, :, hs], k[:, :, hs], v[:, :, hs], dilation, window // dilation)
        outs.append(o)
        lses.append(l)
    wts = jax.nn.softmax(jnp.stack(lses, axis=0), axis=0)
    o = jnp.sum(wts[..., None] * jnp.stack(outs, axis=0).astype(jnp.float32), axis=0)
    return o.reshape(bsz, s, ATT_OUT_WIDTH).astype(q.dtype)


def setup_inputs(seed: int = 0) -> dict:
    key = jax.random.key(seed)
    ks = jax.random.split(key, 24)
    f32 = jnp.float32

    def nrm(k, shape, scale):
        return jax.random.normal(k, shape, f32) * scale

    x = jax.random.normal(ks[0], (BATCH, SEQ, D_MODEL), f32)
    offsets = jax.random.randint(ks[1], (BATCH, 1), 0, 1024, dtype=jnp.int32)
    positions = offsets + jnp.arange(SEQ, dtype=jnp.int32)[None, :]
    a_pow = jax.random.uniform(ks[2], (DEPTH, D_RNN), f32, 0.9, 0.999)
    a0 = a_pow ** (1.0 / LRU_C)
    lru_lambda = jnp.log(a0) - jnp.log1p(-a0)
    return {
        "x": x,
        "positions": positions,
        "ln1_g": 1.0 + nrm(ks[3], (DEPTH, D_MODEL), 0.02),
        "w_in": nrm(ks[4], (DEPTH, D_MODEL, N_IN), D_MODEL ** -0.5),
        "rnn_conv_w": nrm(ks[5], (DEPTH, RNN_CONV_WIDTH, D_RNN), RNN_CONV_WIDTH ** -0.5),
        "rnn_conv_b": nrm(ks[6], (DEPTH, D_RNN), 0.01),
        "lru_wa": nrm(ks[7], (DEPTH, N_LRU_BLOCKS, LRU_BLOCK, LRU_BLOCK), LRU_BLOCK ** -0.5),
        "lru_ba": nrm(ks[8], (DEPTH, D_RNN), 0.01),
        "lru_wx": nrm(ks[9], (DEPTH, N_LRU_BLOCKS, LRU_BLOCK, LRU_BLOCK), LRU_BLOCK ** -0.5),
        "lru_bx": nrm(ks[10], (DEPTH, D_RNN), 0.01),
        "lru_lambda": lru_lambda,
        "q_norm_g": 1.0 + nrm(ks[11], (DEPTH, N_ATT_HEADS, HEAD_DIM), 0.02),
        "k_norm_g": 1.0 + nrm(ks[12], (DEPTH, N_ATT_HEADS, HEAD_DIM), 0.02),
        "proj_rnn": nrm(ks[13], (DEPTH, D_RNN, D_MODEL), D_RNN ** -0.5),
        "proj_attn": nrm(ks[14], (DEPTH, ATT_OUT_WIDTH, D_MODEL), ATT_OUT_WIDTH ** -0.5),
        "w_out": nrm(ks[15], (DEPTH, D_MODEL, D_MODEL), D_MODEL ** -0.5),
        "ln2_g": 1.0 + nrm(ks[16], (DEPTH, D_MODEL), 0.02),
        "w_up": nrm(ks[17], (DEPTH, D_MODEL, 2 * D_FF), D_MODEL ** -0.5),
        "ffn_conv_w": nrm(ks[18], (DEPTH, FFN_CONV_WIDTH, 2 * D_FF), FFN_CONV_WIDTH ** -0.5),
        "ffn_conv_b": nrm(ks[19], (DEPTH, 2 * D_FF), 0.01),
        "w_down": nrm(ks[20], (DEPTH, D_FF, D_MODEL), D_FF ** -0.5),
    }


def reference(x, positions, ln1_g, w_in, rnn_conv_w, rnn_conv_b, lru_wa, lru_ba, lru_wx, lru_bx,
              lru_lambda, q_norm_g, k_norm_g, proj_rnn, proj_attn, w_out, ln2_g, w_up,
              ffn_conv_w, ffn_conv_b, w_down):
    cuts = np.cumsum(IN_SPLITS)[:-1].tolist()
    for l in range(DEPTH):
        h = rms_norm(x, ln1_g[l])
        z = h @ w_in[l]
        xr, gr, q, k, v, g_rnn, g_att = jnp.split(z, cuts, axis=-1)
        xr = causal_depthwise_conv(xr, rnn_conv_w[l], rnn_conv_b[l])
        hr = rg_lru(xr, lru_wa[l], lru_ba[l], lru_wx[l], lru_bx[l], lru_lambda[l])
        y_rnn = hr * jax.nn.gelu(gr)
        y_att = dilated_attention_mixer(q, k, v, positions, q_norm_g[l], k_norm_g[l])
        merged = jax.nn.sigmoid(g_rnn) * (y_rnn @ proj_rnn[l]) + jax.nn.sigmoid(g_att) * (y_att @ proj_attn[l])
        x = x + merged @ w_out[l]
        h2 = rms_norm(x, ln2_g[l])
        u = causal_depthwise_conv(h2 @ w_up[l], ffn_conv_w[l], ffn_conv_b[l])
        u_gate, u_val = jnp.split(u, 2, axis=-1)
        x = x + (jax.nn.gelu(u_gate) * u_val) @ w_down[l]
    return x
```

```python
import functools

import jax
import jax.numpy as jnp
from jax import lax
from jax.experimental import pallas as pl
from jax.experimental.pallas import tpu as pltpu

D_MODEL = 1024
BATCH = 8
SEQ = 2048
DEPTH = 4
M_TOKENS = BATCH * SEQ

D_RNN = D_MODEL
N_LRU_BLOCKS = 8
LRU_BLOCK = D_RNN // N_LRU_BLOCKS
RNN_CONV_WIDTH = 4
LRU_C = 8.0
HEAD_DIM = 128
HEADS_PER_GROUP = 4
DILATED_GROUPS = ((128, 1), (512, 4), (2048, 16))
N_GROUPS = len(DILATED_GROUPS)
N_ATT_HEADS = N_GROUPS * HEADS_PER_GROUP
ATT_WIDTH = N_ATT_HEADS * HEAD_DIM
ATT_OUT_WIDTH = HEADS_PER_GROUP * HEAD_DIM
BLOCK = 128
SPAN = 128
ROPE_THETA = 10000.0
D_FF = 3 * D_MODEL
FFN_CONV_WIDTH = 3
EPS = 1e-6
NEG_INF = -1e30
N_IN = 2 * D_RNN + 3 * ATT_WIDTH + 2 * D_MODEL

F32 = jnp.float32
BF16 = jnp.bfloat16

SUBLANES = 8
BF16_ROWS = 16
SLAB = 512
N_SLABS = N_IN // SLAB
VMEM_LIMIT = 56 * 1024 * 1024

SLAB_XR, SLAB_GR, SLAB_GRNN, SLAB_GATT = 0, 2, 4, 6
SLAB_QKV = (8, 11, 14)
_W_COL_BLOCK = (0, 1, 2, 3, 13, 14, 15, 16, 4, 7, 10, 5, 8, 11, 6, 9, 12)
_GROUP = (0,) * 11 + (1,) * 3 + (2,) * 3
_KIND = (0,) * 8 + (1, 2, 0) * 3
_GAIN_SEL = (0,) * 8 + (0, 3, 0, 1, 4, 0, 2, 5, 0)


def _rms(x):
    return x * lax.rsqrt(jnp.mean(x * x, axis=-1, keepdims=True) + EPS)


def _rope_kernel(pos_ref, inv_ref, cos_ref, sin_ref):
    lane = lax.broadcasted_iota(jnp.int32, (1, HEAD_DIM), 1)
    sign = jnp.where(lane < HEAD_DIM // 2, -1.0, 1.0).astype(F32)
    rows = 256

    def body(c, carry):
        r0 = pl.multiple_of(c * rows, rows)
        ang = pos_ref[pl.ds(r0, rows), :].astype(F32) * inv_ref[...]
        cos_ref[pl.ds(r0, rows), :] = jnp.cos(ang)
        sin_ref[pl.ds(r0, rows), :] = jnp.sin(ang) * sign
        return carry

    lax.fori_loop(0, SEQ // rows, body, 0)


def _rope_tables(pos, inv2):
    out = jax.ShapeDtypeStruct((M_TOKENS, HEAD_DIM), F32)
    return pl.pallas_call(
        _rope_kernel,
        grid=(BATCH,),
        in_specs=[pl.BlockSpec((SEQ, 1), lambda b: (b, 0)),
                  pl.BlockSpec((1, HEAD_DIM), lambda b: (0, 0))],
        out_specs=[pl.BlockSpec((SEQ, HEAD_DIM), lambda b: (b, 0)),
                   pl.BlockSpec((SEQ, HEAD_DIM), lambda b: (b, 0))],
        out_shape=[out, out],
        compiler_params=pltpu.CompilerParams(dimension_semantics=("parallel",)),
        name="rope_tables",
    )(pos, inv2)


IN_ROWS = 512


PLAIN, QUERY, KEY = 0, 1, 2


def _in_proj_kernel(col_ref, branch_ref, gain_ref,
                    x_ref, ln_ref, w_ref, cos_ref, sin_ref, qkg_ref, o_ref, h_scr, p_scr):
    del col_ref, gain_ref
    j = pl.program_id(1)

    @pl.when(j == 0)
    def _():
        def body(q, carry):
            rows = pl.ds(pl.multiple_of(q * BLOCK, BLOCK), BLOCK)
            h_scr[rows, :] = (_rms(x_ref[rows, :]) * ln_ref[...]).astype(BF16)
            return carry

        lax.fori_loop(0, SEQ // BLOCK, body, 0)

    def tile(kind, d):
        n = SEQ // d
        per_stream = IN_ROWS // d
        for c in range(SEQ // IN_ROWS):
            rows = slice(c * IN_ROWS, (c + 1) * IN_ROWS)
            acc = jnp.dot(h_scr[rows, :], w_ref[...], preferred_element_type=F32)
            if kind == PLAIN and d == 1:
                o_ref[0, rows, :] = acc.astype(BF16)
                continue
            for hh in range(HEADS_PER_GROUP):
                cols = slice(hh * HEAD_DIM, (hh + 1) * HEAD_DIM)
                val = acc[:, cols]
                if kind != PLAIN:
                    y = _rms(val) * qkg_ref[0, :, cols]
                    val = y * cos_ref[rows, :] + pltpu.roll(y, HEAD_DIM // 2, axis=1) * sin_ref[rows, :]
                    if kind == QUERY:
                        val = val * HEAD_DIM ** -0.5
                if d == 1:
                    o_ref[0, rows, cols] = val.astype(BF16)
                else:
                    p_scr[hh] = val
                    for r in range(d):
                        dst = slice(r * n + c * per_stream, r * n + (c + 1) * per_stream)
                        o_ref[0, dst, cols] = p_scr[hh, pl.ds(r, per_stream, stride=d), :].astype(BF16)

    branch = branch_ref[j]
    for kind in (PLAIN, QUERY, KEY):
        for g, (_, d) in enumerate(DILATED_GROUPS):
            pl.when(branch == kind * N_GROUPS + g)(functools.partial(tile, kind, d))


def _in_proj(x2, ln_g, w_bf, cos_t, sin_t, qkg):
    branch = tuple(k * N_GROUPS + g for k, g in zip(_KIND, _GROUP))
    tables = [jnp.asarray(t, jnp.int32) for t in (_W_COL_BLOCK, branch, _GAIN_SEL)]
    grid_spec = pltpu.PrefetchScalarGridSpec(
        num_scalar_prefetch=3,
        grid=(BATCH, N_SLABS),
        in_specs=[
            pl.BlockSpec((SEQ, D_MODEL), lambda b, j, c, k, g: (b, 0)),
            pl.BlockSpec((1, D_MODEL), lambda b, j, c, k, g: (0, 0)),
            pl.BlockSpec((D_MODEL, SLAB), lambda b, j, c, k, g: (0, c[j])),
            pl.BlockSpec((SEQ, HEAD_DIM), lambda b, j, c, k, g: (b, 0)),
            pl.BlockSpec((SEQ, HEAD_DIM), lambda b, j, c, k, g: (b, 0)),
            pl.BlockSpec((1, 1, SLAB), lambda b, j, c, k, g: (g[j], 0, 0)),
        ],
        out_specs=pl.BlockSpec((1, SEQ, SLAB), lambda b, j, c, k, g: (j, b, 0)),
        scratch_shapes=[pltpu.VMEM((SEQ, D_MODEL), BF16),
                        pltpu.VMEM((HEADS_PER_GROUP, IN_ROWS, HEAD_DIM), F32)],
    )
    return pl.pallas_call(
        _in_proj_kernel,
        grid_spec=grid_spec,
        out_shape=jax.ShapeDtypeStruct((N_SLABS, M_TOKENS, SLAB), BF16),
        compiler_params=pltpu.CompilerParams(
            dimension_semantics=("parallel", "arbitrary"), vmem_limit_bytes=VMEM_LIMIT),
        name="in_proj",
    )(*tables, x2, ln_g, w_bf, cos_t, sin_t, qkg)


RNN_COLS = 512
RNN_ROWS = 256
RNN_PAD = SUBLANES


def _rnn_kernel(xr_ref, gr_ref, cw_ref, cb_ref, wa_ref, ba_ref, wx_ref, bx_ref, lam_ref,
                o_ref, xs_scr, a_scr, u_scr):
    xs_scr[0:RNN_PAD, :] = jnp.zeros((RNN_PAD, RNN_COLS), F32)

    def stage(c, carry):
        r0 = pl.multiple_of(c * RNN_ROWS, RNN_ROWS)
        xs_scr[pl.ds(RNN_PAD + r0, RNN_ROWS), :] = xr_ref[0, pl.ds(r0, RNN_ROWS), :].astype(F32)
        return carry

    lax.fori_loop(0, SEQ // RNN_ROWS, stage, 0)

    softplus_neg_lam = jax.nn.softplus(-lam_ref[...])

    def gates(c, carry):
        r0 = pl.multiple_of(c * RNN_ROWS, RNN_ROWS)
        win = xs_scr[pl.ds(r0, RNN_ROWS + RNN_PAD), :]
        xc = cb_ref[...]
        for i in range(RNN_CONV_WIDTH):
            back = RNN_CONV_WIDTH - 1 - i
            tap = win if back == 0 else pltpu.roll(win, back, axis=0)
            xc = xc + tap[RNN_PAD:] * cw_ref[i:i + 1, :]
        xc_bf = xc.astype(BF16)
        for nb in range(RNN_COLS // LRU_BLOCK):
            cols = slice(nb * LRU_BLOCK, (nb + 1) * LRU_BLOCK)
            xb = xc_bf[:, cols]
            r = jax.nn.sigmoid(jnp.dot(xb, wa_ref[nb], preferred_element_type=F32) + ba_ref[:, cols])
            ig = jax.nn.sigmoid(jnp.dot(xb, wx_ref[nb], preferred_element_type=F32) + bx_ref[:, cols])
            log_a = -LRU_C * r * softplus_neg_lam[:, cols]
            a = jnp.exp(log_a)
            mult = jnp.sqrt(-jnp.tanh(log_a) * (a * a + 1.0))
            a_scr[pl.ds(r0, RNN_ROWS), cols] = a
            u_scr[pl.ds(r0, RNN_ROWS), cols] = mult * (ig * xc[:, cols])
        return carry

    lax.fori_loop(0, SEQ // RNN_ROWS, gates, 0)

    row = lax.broadcasted_iota(jnp.int32, (SUBLANES, RNN_COLS), 0)

    def scan(t, carry):
        r0 = pl.multiple_of(t * BF16_ROWS, BF16_ROWS)
        a16 = a_scr[pl.ds(r0, BF16_ROWS), :]
        u16 = u_scr[pl.ds(r0, BF16_ROWS), :]
        hs = []
        for half in range(BF16_ROWS // SUBLANES):
            a = a16[half * SUBLANES:(half + 1) * SUBLANES]
            u = u16[half * SUBLANES:(half + 1) * SUBLANES]
            for s in (1, 2, 4):
                a_prev = jnp.where(row >= s, pltpu.roll(a, s, axis=0), 1.0)
                u_prev = jnp.where(row >= s, pltpu.roll(u, s, axis=0), 0.0)
                u = a * u_prev + u
                a = a * a_prev
            h = a * carry + u
            carry = jnp.broadcast_to(h[SUBLANES - 1:SUBLANES, :], (SUBLANES, RNN_COLS))
            hs.append(h)
        gate = jax.nn.gelu(gr_ref[0, pl.ds(r0, BF16_ROWS), :].astype(F32))
        o_ref[pl.ds(r0, BF16_ROWS), :] = (jnp.concatenate(hs, axis=0) * gate).astype(BF16)
        return carry

    lax.fori_loop(0, SEQ // BF16_ROWS, scan, jnp.zeros((SUBLANES, RNN_COLS), F32))


def _rnn(z, conv_w, conv_b, wa_bf, ba, wx_bf, bx, lam):
    n_ct = D_RNN // RNN_COLS
    blocks_per_ct = RNN_COLS // LRU_BLOCK
    vec = pl.BlockSpec((1, RNN_COLS), lambda b, c: (0, c))
    gate_w = pl.BlockSpec((blocks_per_ct, LRU_BLOCK, LRU_BLOCK), lambda b, c: (c, 0, 0))
    return pl.pallas_call(
        _rnn_kernel,
        grid=(BATCH, n_ct),
        in_specs=[
            pl.BlockSpec((1, SEQ, SLAB), lambda b, c: (SLAB_XR + c, b, 0)),
            pl.BlockSpec((1, SEQ, SLAB), lambda b, c: (SLAB_GR + c, b, 0)),
            pl.BlockSpec((RNN_CONV_WIDTH, RNN_COLS), lambda b, c: (0, c)),
            vec, gate_w, vec, gate_w, vec, vec,
        ],
        out_specs=pl.BlockSpec((SEQ, RNN_COLS), lambda b, c: (b, c)),
        out_shape=jax.ShapeDtypeStruct((M_TOKENS, D_RNN), BF16),
        scratch_shapes=[pltpu.VMEM((SEQ + RNN_PAD, RNN_COLS), F32),
                        pltpu.VMEM((SEQ, RNN_COLS), F32),
                        pltpu.VMEM((SEQ, RNN_COLS), F32)],
        compiler_params=pltpu.CompilerParams(
            dimension_semantics=("parallel", "parallel"), vmem_limit_bytes=VMEM_LIMIT),
        name="rnn",
    )(z, z, conv_w, conv_b, wa_bf, ba, wx_bf, bx, lam)


MERGE_ROWS = 256


def _attn_kernel(q0, k0, v0, q1, k1, v1, q2, k2, v2, o_ref, o_scr, l_scr, band_scr, causal_scr):
    qi = lax.broadcasted_iota(jnp.int32, (BLOCK, 2 * BLOCK), 0)
    kj = lax.broadcasted_iota(jnp.int32, (BLOCK, 2 * BLOCK), 1)
    dist = qi + BLOCK - kj
    band_scr[...] = jnp.where((dist >= 0) & (dist <= SPAN), 0.0, NEG_INF).astype(F32)
    qi = lax.broadcasted_iota(jnp.int32, (BLOCK, BLOCK), 0)
    kj = lax.broadcasted_iota(jnp.int32, (BLOCK, BLOCK), 1)
    causal_scr[...] = jnp.where(kj <= qi, 0.0, NEG_INF).astype(F32)
    refs = ((q0, k0, v0), (q1, k1, v1), (q2, k2, v2))

    for g, (_, d) in enumerate(DILATED_GROUPS):
        q_ref, k_ref, v_ref = refs[g]
        bps = SEQ // d // BLOCK

        def attend(i, with_prev, g=g, d=d, bps=bps, q_ref=q_ref, k_ref=k_ref, v_ref=v_ref):
            q_start = pl.multiple_of(i * BLOCK, BLOCK)
            q = q_ref[0, pl.ds(q_start, BLOCK), :]
            if with_prev:
                k_start = pl.multiple_of((i - 1) * BLOCK, BLOCK)
                k = k_ref[0, pl.ds(k_start, 2 * BLOCK), :]
                v = v_ref[0, pl.ds(k_start, 2 * BLOCK), :]
                bias = band_scr[...]
            else:
                k = k_ref[0, pl.ds(q_start, BLOCK), :]
                v = v_ref[0, pl.ds(q_start, BLOCK), :]
                bias = causal_scr[...]
            s = lax.dot_general(q, k, (((1,), (1,)), ((), ())), preferred_element_type=F32) + bias
            m = jnp.max(s, axis=-1, keepdims=True)
            p = jnp.exp(s - m)
            den = jnp.sum(p, axis=-1, keepdims=True)
            out = jnp.dot(p.astype(BF16), v, preferred_element_type=F32) / den
            lse = jnp.broadcast_to(m + jnp.log(den), (BLOCK, HEAD_DIM))
            if d == 1:
                rows = pl.ds(q_start, BLOCK)
            else:
                rows = pl.ds(i // bps + (i % bps) * (BLOCK * d), BLOCK, stride=d)
            o_scr[g, rows, :] = out
            l_scr[g, rows, :] = lse

        def body(i, carry, attend=attend, bps=bps):
            if bps == 1:
                attend(i, False)
            else:
                pl.when(i % bps == 0)(functools.partial(attend, i, False))
                pl.when(i % bps != 0)(functools.partial(attend, i, True))
            return carry

        lax.fori_loop(0, SEQ // BLOCK, body, 0)

    def merge(c, carry):
        rows = pl.ds(pl.multiple_of(c * MERGE_ROWS, MERGE_ROWS), MERGE_ROWS)
        l = [l_scr[g, rows, :] for g in range(N_GROUPS)]
        top = jnp.maximum(jnp.maximum(l[0], l[1]), l[2])
        e = [jnp.exp(lg - top) for lg in l]
        num = e[0] * o_scr[0, rows, :] + e[1] * o_scr[1, rows, :] + e[2] * o_scr[2, rows, :]
        o_ref[rows, :] = (num / (e[0] + e[1] + e[2])).astype(BF16)
        return carry

    lax.fori_loop(0, SEQ // MERGE_ROWS, merge, 0)


def _attn(z):
    specs = []
    for g in range(N_GROUPS):
        for t in range(3):
            specs.append(pl.BlockSpec((1, SEQ, HEAD_DIM),
                                      lambda b, h, s=SLAB_QKV[g] + t: (s, b, h)))
    return pl.pallas_call(
        _attn_kernel,
        grid=(BATCH, HEADS_PER_GROUP),
        in_specs=specs,
        out_specs=pl.BlockSpec((SEQ, HEAD_DIM), lambda b, h: (b, h)),
        out_shape=jax.ShapeDtypeStruct((M_TOKENS, ATT_OUT_WIDTH), BF16),
        scratch_shapes=[pltpu.VMEM((N_GROUPS, SEQ, HEAD_DIM), F32),
                        pltpu.VMEM((N_GROUPS, SEQ, HEAD_DIM), F32),
                        pltpu.VMEM((BLOCK, 2 * BLOCK), F32),
                        pltpu.VMEM((BLOCK, BLOCK), F32)],
        compiler_params=pltpu.CompilerParams(
            dimension_semantics=("parallel", "parallel"), vmem_limit_bytes=VMEM_LIMIT),
        name="attn",
    )(*([z] * 9))


MO_ROWS = 512


def _merge_out_kernel(yr_ref, ya_ref, grnn_ref, gatt_ref, x_ref, pr_ref, pa_ref, wo_ref, ln2_ref,
                      xo_ref, h2_ref):
    t_rnn = jnp.dot(yr_ref[...], pr_ref[...], preferred_element_type=F32)
    t_att = jnp.dot(ya_ref[...], pa_ref[...], preferred_element_type=F32)
    halves = []
    for hf in range(D_MODEL // SLAB):
        cols = slice(hf * SLAB, (hf + 1) * SLAB)
        g_rnn = jax.nn.sigmoid(grnn_ref[hf].astype(F32))
        g_att = jax.nn.sigmoid(gatt_ref[hf].astype(F32))
        halves.append((g_rnn * t_rnn[:, cols] + g_att * t_att[:, cols]).astype(BF16))
    merged = jnp.concatenate(halves, axis=1)
    xn = x_ref[...] + jnp.dot(merged, wo_ref[...], preferred_element_type=F32)
    xo_ref[...] = xn
    h2_ref[...] = (_rms(xn) * ln2_ref[...]).astype(BF16)


def _merge_out(y_rnn, y_att, z, x2, pr_bf, pa_bf, wo_bf, ln2):
    slabs_per_gate = D_MODEL // SLAB
    full = lambda shape: pl.BlockSpec(shape, lambda i: (0, 0))
    return pl.pallas_call(
        _merge_out_kernel,
        grid=(M_TOKENS // MO_ROWS,),
        in_specs=[
            pl.BlockSpec((MO_ROWS, D_RNN), lambda i: (i, 0)),
            pl.BlockSpec((MO_ROWS, ATT_OUT_WIDTH), lambda i: (i, 0)),
            pl.BlockSpec((slabs_per_gate, MO_ROWS, SLAB), lambda i: (SLAB_GRNN // slabs_per_gate, i, 0)),
            pl.BlockSpec((slabs_per_gate, MO_ROWS, SLAB), lambda i: (SLAB_GATT // slabs_per_gate, i, 0)),
            pl.BlockSpec((MO_ROWS, D_MODEL), lambda i: (i, 0)),
            full((D_RNN, D_MODEL)), full((ATT_OUT_WIDTH, D_MODEL)), full((D_MODEL, D_MODEL)),
            full((1, D_MODEL)),
        ],
        out_specs=[pl.BlockSpec((MO_ROWS, D_MODEL), lambda i: (i, 0)),
                   pl.BlockSpec((MO_ROWS, D_MODEL), lambda i: (i, 0))],
        out_shape=[jax.ShapeDtypeStruct((M_TOKENS, D_MODEL), F32),
                   jax.ShapeDtypeStruct((M_TOKENS, D_MODEL), BF16)],
        compiler_params=pltpu.CompilerParams(
            dimension_semantics=("parallel",), vmem_limit_bytes=VMEM_LIMIT),
        name="merge_out",
    )(y_rnn, y_att, z, z, x2, pr_bf, pa_bf, wo_bf, ln2)


FFN_ROWS = 1024
FFN_COLS = 512
FFN_CHUNK = 256
FFN_HALO = BF16_ROWS


def _ffn_kernel(h2_ref, halo_ref, wg_ref, wv_ref, cwg_ref, cwv_ref, cbg_ref, cbv_ref, wd_ref, x_ref,
                o_ref, hbuf):
    i = pl.program_id(0)
    k = pl.program_id(1)

    @pl.when(k == 0)
    def _():
        seq_start = i % (SEQ // FFN_ROWS) == 0

        @pl.when(seq_start)
        def _():
            hbuf[0:FFN_HALO, :] = jnp.zeros((FFN_HALO, D_MODEL), BF16)

        @pl.when(jnp.logical_not(seq_start))
        def _():
            hbuf[0:FFN_HALO, :] = halo_ref[...]

        hbuf[FFN_HALO:, :] = h2_ref[...]
        o_ref[...] = x_ref[...]

    def conv(u, cw_ref, cb_ref):
        out = cb_ref[...]
        for t in range(FFN_CONV_WIDTH):
            back = FFN_CONV_WIDTH - 1 - t
            tap = u if back == 0 else pltpu.roll(u, back, axis=0)
            out = out + tap[FFN_HALO:] * cw_ref[t:t + 1, :]
        return out

    def chunk(c, carry):
        r0 = pl.multiple_of(c * FFN_CHUNK, FFN_CHUNK)
        hb = hbuf[pl.ds(r0, FFN_CHUNK + FFN_HALO), :]
        gate = conv(jnp.dot(hb, wg_ref[...], preferred_element_type=F32), cwg_ref, cbg_ref)
        val = conv(jnp.dot(hb, wv_ref[...], preferred_element_type=F32), cwv_ref, cbv_ref)
        act = (jax.nn.gelu(gate) * val).astype(BF16)
        o_ref[pl.ds(r0, FFN_CHUNK), :] += jnp.dot(act, wd_ref[...], preferred_element_type=F32)
        return carry

    lax.fori_loop(0, FFN_ROWS // FFN_CHUNK, chunk, 0)


def _ffn(h2, x2, wup_bf, conv_w, conv_b, wd_bf):
    n_k = D_FF // FFN_COLS
    halo_blocks = FFN_ROWS // FFN_HALO
    return pl.pallas_call(
        _ffn_kernel,
        grid=(M_TOKENS // FFN_ROWS, n_k),
        in_specs=[
            pl.BlockSpec((FFN_ROWS, D_MODEL), lambda i, k: (i, 0)),
            pl.BlockSpec((FFN_HALO, D_MODEL), lambda i, k: (jnp.maximum(i * halo_blocks - 1, 0), 0)),
            pl.BlockSpec((D_MODEL, FFN_COLS), lambda i, k: (0, k)),
            pl.BlockSpec((D_MODEL, FFN_COLS), lambda i, k: (0, n_k + k)),
            pl.BlockSpec((FFN_CONV_WIDTH, FFN_COLS), lambda i, k: (0, k)),
            pl.BlockSpec((FFN_CONV_WIDTH, FFN_COLS), lambda i, k: (0, n_k + k)),
            pl.BlockSpec((1, FFN_COLS), lambda i, k: (0, k)),
            pl.BlockSpec((1, FFN_COLS), lambda i, k: (0, n_k + k)),
            pl.BlockSpec((FFN_COLS, D_MODEL), lambda i, k: (k, 0)),
            pl.BlockSpec((FFN_ROWS, D_MODEL), lambda i, k: (i, 0)),
        ],
        out_specs=pl.BlockSpec((FFN_ROWS, D_MODEL), lambda i, k: (i, 0)),
        out_shape=jax.ShapeDtypeStruct((M_TOKENS, D_MODEL), F32),
        scratch_shapes=[pltpu.VMEM((FFN_ROWS + FFN_HALO, D_MODEL), BF16)],
        compiler_params=pltpu.CompilerParams(
            dimension_semantics=("parallel", "arbitrary"), vmem_limit_bytes=VMEM_LIMIT),
        name="ffn",
    )(h2, h2, wup_bf, wup_bf, conv_w, conv_w, conv_b, conv_b, wd_bf, x2)


def kernel(x, positions, ln1_g, w_in, rnn_conv_w, rnn_conv_b, lru_wa, lru_ba, lru_wx, lru_bx, lru_lambda, q_norm_g, k_norm_g, proj_rnn, proj_attn, w_out, ln2_g, w_up, ffn_conv_w, ffn_conv_b, w_down):
    x2 = x.reshape(M_TOKENS, D_MODEL)
    half = HEAD_DIM // 2
    inv_freq = ROPE_THETA ** (-jnp.arange(half, dtype=F32) / half)
    inv2 = jnp.concatenate([inv_freq, inv_freq]).reshape(1, HEAD_DIM)
    cos_t, sin_t = _rope_tables(positions.reshape(M_TOKENS, 1), inv2)

    row = lambda v: v.reshape(1, -1)
    for l in range(DEPTH):
        qkg = jnp.concatenate([q_norm_g[l].reshape(N_GROUPS, 1, SLAB),
                               k_norm_g[l].reshape(N_GROUPS, 1, SLAB)], axis=0)
        z = _in_proj(x2, row(ln1_g[l]), w_in[l].astype(BF16), cos_t, sin_t, qkg)
        y_rnn = _rnn(z, rnn_conv_w[l], row(rnn_conv_b[l]), lru_wa[l].astype(BF16), row(lru_ba[l]),
                     lru_wx[l].astype(BF16), row(lru_bx[l]), row(lru_lambda[l]))
        y_att = _attn(z)
        x2, h2 = _merge_out(y_rnn, y_att, z, x2, proj_rnn[l].astype(BF16), proj_attn[l].astype(BF16),
                            w_out[l].astype(BF16), row(ln2_g[l]))
        x2 = _ffn(h2, x2, w_up[l].astype(BF16), ffn_conv_w[l], row(ffn_conv_b[l]),
                  w_down[l].astype(BF16))
    return x2.reshape(BATCH, SEQ, D_MODEL)
```

```python
import functools

import jax
import jax.numpy as jnp
from jax import lax
from jax.experimental import pallas as pl
from jax.experimental.pallas import tpu as pltpu

D_MODEL = 1024
BATCH = 8
SEQ = 2048
DEPTH = 4
M_TOKENS = BATCH * SEQ

D_RNN = D_MODEL
N_LRU_BLOCKS = 8
LRU_BLOCK = D_RNN // N_LRU_BLOCKS
RNN_CONV_WIDTH = 4
LRU_C = 8.0
HEAD_DIM = 128
HEADS_PER_GROUP = 4
DILATED_GROUPS = ((128, 1), (512, 4), (2048, 16))
N_GROUPS = len(DILATED_GROUPS)
N_ATT_HEADS = N_GROUPS * HEADS_PER_GROUP
ATT_WIDTH = N_ATT_HEADS * HEAD_DIM
ATT_OUT_WIDTH = HEADS_PER_GROUP * HEAD_DIM
BLOCK = 128
SPAN = 128
ROPE_THETA = 10000.0
D_FF = 3 * D_MODEL
FFN_CONV_WIDTH = 3
EPS = 1e-6
NEG_INF = -1e30
N_IN = 2 * D_RNN + 3 * ATT_WIDTH + 2 * D_MODEL

F32 = jnp.float32
BF16 = jnp.bfloat16

SUBLANES = 8
BF16_ROWS = 16
SLAB = 512
N_SLABS = N_IN // SLAB
VMEM_LIMIT = 56 * 1024 * 1024

SLAB_XR, SLAB_GR, SLAB_GRNN, SLAB_GATT = 0, 2, 4, 6
SLAB_QKV = (8, 11, 14)
_W_COL_BLOCK = (0, 1, 2, 3, 13, 14, 15, 16, 4, 7, 10, 5, 8, 11, 6, 9, 12)
_GROUP = (0,) * 11 + (1,) * 3 + (2,) * 3
_KIND = (0,) * 8 + (1, 2, 0) * 3
_GAIN_SEL = (0,) * 8 + (0, 3, 0, 1, 4, 0, 2, 5, 0)


def _rms(x):
    return x * lax.rsqrt(jnp.mean(x * x, axis=-1, keepdims=True) + EPS)


def _rope_kernel(pos_ref, inv_ref, cos_ref, sin_ref):
    lane = lax.broadcasted_iota(jnp.int32, (1, HEAD_DIM), 1)
    sign = jnp.where(lane < HEAD_DIM // 2, -1.0, 1.0).astype(F32)
    rows = 256

    def body(c, carry):
        r0 = pl.multiple_of(c * rows, rows)
        ang = pos_ref[pl.ds(r0, rows), :].astype(F32) * inv_ref[...]
        cos_ref[pl.ds(r0, rows), :] = jnp.cos(ang)
        sin_ref[pl.ds(r0, rows), :] = jnp.sin(ang) * sign
        return carry

    lax.fori_loop(0, SEQ // rows, body, 0)


def _rope_tables(pos, inv2):
    out = jax.ShapeDtypeStruct((M_TOKENS, HEAD_DIM), F32)
    return pl.pallas_call(
        _rope_kernel,
        grid=(BATCH,),
        in_specs=[pl.BlockSpec((SEQ, 1), lambda b: (b, 0)),
                  pl.BlockSpec((1, HEAD_DIM), lambda b: (0, 0))],
        out_specs=[pl.BlockSpec((SEQ, HEAD_DIM), lambda b: (b, 0)),
                   pl.BlockSpec((SEQ, HEAD_DIM), lambda b: (b, 0))],
        out_shape=[out, out],
        compiler_params=pltpu.CompilerParams(dimension_semantics=("parallel",)),
        name="rope_tables",
    )(pos, inv2)


IN_ROWS = 512


PLAIN, QUERY, KEY = 0, 1, 2


def _in_proj_kernel(col_ref, kind_ref, group_ref, gain_ref,
                    x_ref, ln_ref, w_ref, cos_ref, sin_ref, qkg_ref, o_ref, h_scr, p_scr):
    del col_ref, gain_ref
    j = pl.program_id(1)

    @pl.when(j == 0)
    def _():
        def body(q, carry):
            rows = pl.ds(pl.multiple_of(q * BLOCK, BLOCK), BLOCK)
            h_scr[rows, :] = (_rms(x_ref[rows, :]) * ln_ref[...]).astype(BF16)
            return carry

        lax.fori_loop(0, SEQ // BLOCK, body, 0)

    kind = kind_ref[j]
    group = group_ref[j]
    heads = [slice(hh * HEAD_DIM, (hh + 1) * HEAD_DIM) for hh in range(HEADS_PER_GROUP)]

    def chunk(c, carry):
        rows = pl.ds(pl.multiple_of(c * IN_ROWS, IN_ROWS), IN_ROWS)
        acc = jnp.dot(h_scr[rows, :], w_ref[...], preferred_element_type=F32)

        @pl.when(kind == PLAIN)
        def _():
            for hh, cols in enumerate(heads):
                p_scr[hh] = acc[:, cols]

        @pl.when(kind != PLAIN)
        def _():
            for hh, cols in enumerate(heads):
                y = _rms(acc[:, cols]) * qkg_ref[0, 0:1, cols]
                rot = y * cos_ref[rows, :] + pltpu.roll(y, HEAD_DIM // 2, axis=1) * sin_ref[rows, :]
                p_scr[hh] = rot * qkg_ref[0, 1:2, cols]

        for g, (_, d) in enumerate(DILATED_GROUPS):
            @pl.when(group == g)
            def _(d=d):
                n = SEQ // d
                per_stream = IN_ROWS // d
                for hh, cols in enumerate(heads):
                    for r in range(d):
                        dst = pl.ds(pl.multiple_of(r * n + c * per_stream, per_stream), per_stream)
                        src = pl.ds(r, per_stream, stride=d) if d > 1 else slice(None)
                        o_ref[0, dst, cols] = p_scr[hh, src, :].astype(BF16)

        return carry

    lax.fori_loop(0, SEQ // IN_ROWS, chunk, 0)


def _in_proj(x2, ln_g, w_bf, cos_t, sin_t, qkg):
    tables = [jnp.asarray(t, jnp.int32) for t in (_W_COL_BLOCK, _KIND, _GROUP, _GAIN_SEL)]
    grid_spec = pltpu.PrefetchScalarGridSpec(
        num_scalar_prefetch=4,
        grid=(BATCH, N_SLABS),
        in_specs=[
            pl.BlockSpec((SEQ, D_MODEL), lambda b, j, c, k, o, g: (b, 0)),
            pl.BlockSpec((1, D_MODEL), lambda b, j, c, k, o, g: (0, 0)),
            pl.BlockSpec((D_MODEL, SLAB), lambda b, j, c, k, o, g: (0, c[j])),
            pl.BlockSpec((SEQ, HEAD_DIM), lambda b, j, c, k, o, g: (b, 0)),
            pl.BlockSpec((SEQ, HEAD_DIM), lambda b, j, c, k, o, g: (b, 0)),
            pl.BlockSpec((1, 2, SLAB), lambda b, j, c, k, o, g: (g[j], 0, 0)),
        ],
        out_specs=pl.BlockSpec((1, SEQ, SLAB), lambda b, j, c, k, o, g: (j, b, 0)),
        scratch_shapes=[pltpu.VMEM((SEQ, D_MODEL), BF16),
                        pltpu.VMEM((HEADS_PER_GROUP, IN_ROWS, HEAD_DIM), F32)],
    )
    return pl.pallas_call(
        _in_proj_kernel,
        grid_spec=grid_spec,
        out_shape=jax.ShapeDtypeStruct((N_SLABS, M_TOKENS, SLAB), BF16),
        compiler_params=pltpu.CompilerParams(
            dimension_semantics=("parallel", "arbitrary"), vmem_limit_bytes=VMEM_LIMIT),
        name="in_proj",
    )(*tables, x2, ln_g, w_bf, cos_t, sin_t, qkg)


RNN_COLS = 512
RNN_ROWS = 256
RNN_PAD = SUBLANES


def _rnn_kernel(xr_ref, gr_ref, cw_ref, cb_ref, wa_ref, ba_ref, wx_ref, bx_ref, lam_ref,
                o_ref, xs_scr, a_scr, u_scr):
    xs_scr[0:RNN_PAD, :] = jnp.zeros((RNN_PAD, RNN_COLS), F32)

    def stage(c, carry):
        r0 = pl.multiple_of(c * RNN_ROWS, RNN_ROWS)
        xs_scr[pl.ds(RNN_PAD + r0, RNN_ROWS), :] = xr_ref[0, pl.ds(r0, RNN_ROWS), :].astype(F32)
        return carry

    lax.fori_loop(0, SEQ // RNN_ROWS, stage, 0)

    softplus_neg_lam = jax.nn.softplus(-lam_ref[...])

    def gates(c, carry):
        r0 = pl.multiple_of(c * RNN_ROWS, RNN_ROWS)
        win = xs_scr[pl.ds(r0, RNN_ROWS + RNN_PAD), :]
        xc = cb_ref[...]
        for i in range(RNN_CONV_WIDTH):
            back = RNN_CONV_WIDTH - 1 - i
            tap = win if back == 0 else pltpu.roll(win, back, axis=0)
            xc = xc + tap[RNN_PAD:] * cw_ref[i:i + 1, :]
        xc_bf = xc.astype(BF16)
        for nb in range(RNN_COLS // LRU_BLOCK):
            cols = slice(nb * LRU_BLOCK, (nb + 1) * LRU_BLOCK)
            xb = xc_bf[:, cols]
            r = jax.nn.sigmoid(jnp.dot(xb, wa_ref[nb], preferred_element_type=F32) + ba_ref[:, cols])
            ig = jax.nn.sigmoid(jnp.dot(xb, wx_ref[nb], preferred_element_type=F32) + bx_ref[:, cols])
            log_a = -LRU_C * r * softplus_neg_lam[:, cols]
            a = jnp.exp(log_a)
            mult = jnp.sqrt(-jnp.tanh(log_a) * (a * a + 1.0))
            a_scr[pl.ds(r0, RNN_ROWS), cols] = a
            u_scr[pl.ds(r0, RNN_ROWS), cols] = mult * (ig * xc[:, cols])
        return carry

    lax.fori_loop(0, SEQ // RNN_ROWS, gates, 0)

    row = lax.broadcasted_iota(jnp.int32, (SUBLANES, RNN_COLS), 0)

    def scan(t, carry):
        r0 = pl.multiple_of(t * BF16_ROWS, BF16_ROWS)
        a16 = a_scr[pl.ds(r0, BF16_ROWS), :]
        u16 = u_scr[pl.ds(r0, BF16_ROWS), :]
        hs = []
        for half in range(BF16_ROWS // SUBLANES):
            a = a16[half * SUBLANES:(half + 1) * SUBLANES]
            u = u16[half * SUBLANES:(half + 1) * SUBLANES]
            for s in (1, 2, 4):
                a_prev = jnp.where(row >= s, pltpu.roll(a, s, axis=0), 1.0)
                u_prev = jnp.where(row >= s, pltpu.roll(u, s, axis=0), 0.0)
                u = a * u_prev + u
                a = a * a_prev
            h = a * carry + u
            carry = jnp.broadcast_to(h[SUBLANES - 1:SUBLANES, :], (SUBLANES, RNN_COLS))
            hs.append(h)
        gate = jax.nn.gelu(gr_ref[0, pl.ds(r0, BF16_ROWS), :].astype(F32))
        o_ref[pl.ds(r0, BF16_ROWS), :] = (jnp.concatenate(hs, axis=0) * gate).astype(BF16)
        return carry

    lax.fori_loop(0, SEQ // BF16_ROWS, scan, jnp.zeros((SUBLANES, RNN_COLS), F32))


def _rnn(z, conv_w, conv_b, wa_bf, ba, wx_bf, bx, lam):
    n_ct = D_RNN // RNN_COLS
    blocks_per_ct = RNN_COLS // LRU_BLOCK
    vec = pl.BlockSpec((1, RNN_COLS), lambda b, c: (0, c))
    gate_w = pl.BlockSpec((blocks_per_ct, LRU_BLOCK, LRU_BLOCK), lambda b, c: (c, 0, 0))
    return pl.pallas_call(
        _rnn_kernel,
        grid=(BATCH, n_ct),
        in_specs=[
            pl.BlockSpec((1, SEQ, SLAB), lambda b, c: (SLAB_XR + c, b, 0)),
            pl.BlockSpec((1, SEQ, SLAB), lambda b, c: (SLAB_GR + c, b, 0)),
            pl.BlockSpec((RNN_CONV_WIDTH, RNN_COLS), lambda b, c: (0, c)),
            vec, gate_w, vec, gate_w, vec, vec,
        ],
        out_specs=pl.BlockSpec((SEQ, RNN_COLS), lambda b, c: (b, c)),
        out_shape=jax.ShapeDtypeStruct((M_TOKENS, D_RNN), BF16),
        scratch_shapes=[pltpu.VMEM((SEQ + RNN_PAD, RNN_COLS), F32),
                        pltpu.VMEM((SEQ, RNN_COLS), F32),
                        pltpu.VMEM((SEQ, RNN_COLS), F32)],
        compiler_params=pltpu.CompilerParams(
            dimension_semantics=("parallel", "parallel"), vmem_limit_bytes=VMEM_LIMIT),
        name="rnn",
    )(z, z, conv_w, conv_b, wa_bf, ba, wx_bf, bx, lam)


MERGE_ROWS = 256
ATT_HEADS = 2
ATT_UNROLL = {1: 3, 4: 4, 16: 4}


def _attn_kernel(q0, k0, v0, q1, k1, v1, q2, k2, v2, o_ref, o_scr, l_scr, band_scr, causal_scr):
    qi = lax.broadcasted_iota(jnp.int32, (BLOCK, 2 * BLOCK), 0)
    kj = lax.broadcasted_iota(jnp.int32, (BLOCK, 2 * BLOCK), 1)
    dist = qi + BLOCK - kj
    band_scr[...] = jnp.where((dist >= 0) & (dist <= SPAN), 0.0, NEG_INF).astype(F32)
    qi = lax.broadcasted_iota(jnp.int32, (BLOCK, BLOCK), 0)
    kj = lax.broadcasted_iota(jnp.int32, (BLOCK, BLOCK), 1)
    causal_scr[...] = jnp.where(kj <= qi, 0.0, NEG_INF).astype(F32)
    refs = ((q0, k0, v0), (q1, k1, v1), (q2, k2, v2))

    def attend(g, i, with_prev):
        q_ref, k_ref, v_ref = refs[g]
        d = DILATED_GROUPS[g][1]
        bps = SEQ // d // BLOCK
        q_start = pl.multiple_of(i * BLOCK, BLOCK)
        if with_prev:
            k_rows = pl.ds(pl.multiple_of((i - 1) * BLOCK, BLOCK), 2 * BLOCK)
            bias = band_scr[...]
        else:
            k_rows = pl.ds(q_start, BLOCK)
            bias = causal_scr[...]
        if d == 1:
            rows = pl.ds(q_start, BLOCK)
        else:
            rows = pl.ds(i // bps + (i % bps) * (BLOCK * d), BLOCK, stride=d)
        for hh in range(ATT_HEADS):
            cols = slice(hh * HEAD_DIM, (hh + 1) * HEAD_DIM)
            q = q_ref[0, pl.ds(q_start, BLOCK), cols]
            k = k_ref[0, k_rows, cols]
            v = v_ref[0, k_rows, cols]
            s = lax.dot_general(q, k, (((1,), (1,)), ((), ())), preferred_element_type=F32) + bias
            m = jnp.max(s, axis=-1, keepdims=True)
            p = jnp.exp(s - m)
            den = jnp.sum(p, axis=-1, keepdims=True)
            out = jnp.dot(p.astype(BF16), v, preferred_element_type=F32) / den
            o_scr[g * ATT_HEADS + hh, rows, :] = out
            l_scr[g * ATT_HEADS + hh, rows, :] = jnp.broadcast_to(m + jnp.log(den), (BLOCK, HEAD_DIM))

    n_blocks = SEQ // BLOCK
    for g, (_, d) in enumerate(DILATED_GROUPS):
        bps = n_blocks // d
        unroll = ATT_UNROLL[d]
        if bps == 1:
            def body(t, carry, g=g, unroll=unroll):
                for u in range(unroll):
                    attend(g, t * unroll + u, False)
                return carry

            lax.fori_loop(0, n_blocks // unroll, body, 0)
        elif bps == unroll:
            def body(r, carry, g=g, bps=bps):
                attend(g, r * bps, False)
                for u in range(1, bps):
                    attend(g, r * bps + u, True)
                return carry

            lax.fori_loop(0, d, body, 0)
        else:
            attend(g, 0, False)

            def body(t, carry, g=g, unroll=unroll):
                for u in range(unroll):
                    attend(g, 1 + t * unroll + u, True)
                return carry

            lax.fori_loop(0, (bps - 1) // unroll, body, 0)

    def merge(c, carry):
        rows = pl.ds(pl.multiple_of(c * MERGE_ROWS, MERGE_ROWS), MERGE_ROWS)
        for hh in range(ATT_HEADS):
            l = [l_scr[g * ATT_HEADS + hh, rows, :] for g in range(N_GROUPS)]
            o = [o_scr[g * ATT_HEADS + hh, rows, :] for g in range(N_GROUPS)]
            top = jnp.maximum(jnp.maximum(l[0], l[1]), l[2])
            e = [jnp.exp(lg - top) for lg in l]
            num = e[0] * o[0] + e[1] * o[1] + e[2] * o[2]
            o_ref[rows, hh * HEAD_DIM:(hh + 1) * HEAD_DIM] = (num / (e[0] + e[1] + e[2])).astype(BF16)
        return carry

    lax.fori_loop(0, SEQ // MERGE_ROWS, merge, 0)


def _attn(z):
    width = ATT_HEADS * HEAD_DIM
    specs = []
    for g in range(N_GROUPS):
        for t in range(3):
            specs.append(pl.BlockSpec((1, SEQ, width), lambda b, h, s=SLAB_QKV[g] + t: (s, b, h)))
    return pl.pallas_call(
        _attn_kernel,
        grid=(BATCH, HEADS_PER_GROUP // ATT_HEADS),
        in_specs=specs,
        out_specs=pl.BlockSpec((SEQ, width), lambda b, h: (b, h)),
        out_shape=jax.ShapeDtypeStruct((M_TOKENS, ATT_OUT_WIDTH), BF16),
        scratch_shapes=[pltpu.VMEM((N_GROUPS * ATT_HEADS, SEQ, HEAD_DIM), F32),
                        pltpu.VMEM((N_GROUPS * ATT_HEADS, SEQ, HEAD_DIM), F32),
                        pltpu.VMEM((BLOCK, 2 * BLOCK), F32),
                        pltpu.VMEM((BLOCK, BLOCK), F32)],
        compiler_params=pltpu.CompilerParams(
            dimension_semantics=("parallel", "parallel"), vmem_limit_bytes=VMEM_LIMIT),
        name="attn",
    )(*([z] * 9))


MO_ROWS = 512


def _merge_out_kernel(yr_ref, ya_ref, grnn_ref, gatt_ref, x_ref, pr_ref, pa_ref, wo_ref, ln2_ref,
                      xo_ref, h2_ref):
    t_rnn = jnp.dot(yr_ref[...], pr_ref[...], preferred_element_type=F32)
    t_att = jnp.dot(ya_ref[...], pa_ref[...], preferred_element_type=F32)
    halves = []
    for hf in range(D_MODEL // SLAB):
        cols = slice(hf * SLAB, (hf + 1) * SLAB)
        g_rnn = jax.nn.sigmoid(grnn_ref[hf].astype(F32))
        g_att = jax.nn.sigmoid(gatt_ref[hf].astype(F32))
        halves.append((g_rnn * t_rnn[:, cols] + g_att * t_att[:, cols]).astype(BF16))
    merged = jnp.concatenate(halves, axis=1)
    xn = x_ref[...] + jnp.dot(merged, wo_ref[...], preferred_element_type=F32)
    xo_ref[...] = xn
    h2_ref[...] = (_rms(xn) * ln2_ref[...]).astype(BF16)


def _merge_out(y_rnn, y_att, z, x2, pr_bf, pa_bf, wo_bf, ln2):
    slabs_per_gate = D_MODEL // SLAB
    full = lambda shape: pl.BlockSpec(shape, lambda i: (0, 0))
    return pl.pallas_call(
        _merge_out_kernel,
        grid=(M_TOKENS // MO_ROWS,),
        in_specs=[
            pl.BlockSpec((MO_ROWS, D_RNN), lambda i: (i, 0)),
            pl.BlockSpec((MO_ROWS, ATT_OUT_WIDTH), lambda i: (i, 0)),
            pl.BlockSpec((slabs_per_gate, MO_ROWS, SLAB), lambda i: (SLAB_GRNN // slabs_per_gate, i, 0)),
            pl.BlockSpec((slabs_per_gate, MO_ROWS, SLAB), lambda i: (SLAB_GATT // slabs_per_gate, i, 0)),
            pl.BlockSpec((MO_ROWS, D_MODEL), lambda i: (i, 0)),
            full((D_RNN, D_MODEL)), full((ATT_OUT_WIDTH, D_MODEL)), full((D_MODEL, D_MODEL)),
            full((1, D_MODEL)),
        ],
        out_specs=[pl.BlockSpec((MO_ROWS, D_MODEL), lambda i: (i, 0)),
                   pl.BlockSpec((MO_ROWS, D_MODEL), lambda i: (i, 0))],
        out_shape=[jax.ShapeDtypeStruct((M_TOKENS, D_MODEL), F32),
                   jax.ShapeDtypeStruct((M_TOKENS, D_MODEL), BF16)],
        compiler_params=pltpu.CompilerParams(
            dimension_semantics=("parallel",), vmem_limit_bytes=VMEM_LIMIT),
        name="merge_out",
    )(y_rnn, y_att, z, z, x2, pr_bf, pa_bf, wo_bf, ln2)


FFN_ROWS = 1024
FFN_COLS = 512
FFN_CHUNK = 256
FFN_HALO = BF16_ROWS


def _ffn_kernel(h2_ref, halo_ref, wg_ref, wv_ref, cwg_ref, cwv_ref, cbg_ref, cbv_ref, wd_ref, x_ref,
                o_ref, hbuf):
    i = pl.program_id(0)
    k = pl.program_id(1)

    @pl.when(k == 0)
    def _():
        seq_start = i % (SEQ // FFN_ROWS) == 0

        @pl.when(seq_start)
        def _():
            hbuf[0:FFN_HALO, :] = jnp.zeros((FFN_HALO, D_MODEL), BF16)

        @pl.when(jnp.logical_not(seq_start))
        def _():
            hbuf[0:FFN_HALO, :] = halo_ref[...]

        hbuf[FFN_HALO:, :] = h2_ref[...]
        o_ref[...] = x_ref[...]

    def conv(u, cw_ref, cb_ref):
        out = cb_ref[...]
        for t in range(FFN_CONV_WIDTH):
            back = FFN_CONV_WIDTH - 1 - t
            tap = u if back == 0 else pltpu.roll(u, back, axis=0)
            out = out + tap[FFN_HALO:] * cw_ref[t:t + 1, :]
        return out

    def chunk(c, carry):
        r0 = pl.multiple_of(c * FFN_CHUNK, FFN_CHUNK)
        hb = hbuf[pl.ds(r0, FFN_CHUNK + FFN_HALO), :]
        gate = conv(jnp.dot(hb, wg_ref[...], preferred_element_type=F32), cwg_ref, cbg_ref)
        val = conv(jnp.dot(hb, wv_ref[...], preferred_element_type=F32), cwv_ref, cbv_ref)
        act = (jax.nn.gelu(gate) * val).astype(BF16)
        o_ref[pl.ds(r0, FFN_CHUNK), :] += jnp.dot(act, wd_ref[...], preferred_element_type=F32)
        return carry

    lax.fori_loop(0, FFN_ROWS // FFN_CHUNK, chunk, 0)


def _ffn(h2, x2, wup_bf, conv_w, conv_b, wd_bf):
    n_k = D_FF // FFN_COLS
    halo_blocks = FFN_ROWS // FFN_HALO
    return pl.pallas_call(
        _ffn_kernel,
        grid=(M_TOKENS // FFN_ROWS, n_k),
        in_specs=[
            pl.BlockSpec((FFN_ROWS, D_MODEL), lambda i, k: (i, 0)),
            pl.BlockSpec((FFN_HALO, D_MODEL), lambda i, k: (jnp.maximum(i * halo_blocks - 1, 0), 0)),
            pl.BlockSpec((D_MODEL, FFN_COLS), lambda i, k: (0, k)),
            pl.BlockSpec((D_MODEL, FFN_COLS), lambda i, k: (0, n_k + k)),
            pl.BlockSpec((FFN_CONV_WIDTH, FFN_COLS), lambda i, k: (0, k)),
            pl.BlockSpec((FFN_CONV_WIDTH, FFN_COLS), lambda i, k: (0, n_k + k)),
            pl.BlockSpec((1, FFN_COLS), lambda i, k: (0, k)),
            pl.BlockSpec((1, FFN_COLS), lambda i, k: (0, n_k + k)),
            pl.BlockSpec((FFN_COLS, D_MODEL), lambda i, k: (k, 0)),
            pl.BlockSpec((FFN_ROWS, D_MODEL), lambda i, k: (i, 0)),
        ],
        out_specs=pl.BlockSpec((FFN_ROWS, D_MODEL), lambda i, k: (i, 0)),
        out_shape=jax.ShapeDtypeStruct((M_TOKENS, D_MODEL), F32),
        scratch_shapes=[pltpu.VMEM((FFN_ROWS + FFN_HALO, D_MODEL), BF16)],
        compiler_params=pltpu.CompilerParams(
            dimension_semantics=("parallel", "arbitrary"), vmem_limit_bytes=VMEM_LIMIT),
        name="ffn",
    )(h2, h2, wup_bf, wup_bf, conv_w, conv_w, conv_b, conv_b, wd_bf, x2)


def kernel(x, positions, ln1_g, w_in, rnn_conv_w, rnn_conv_b, lru_wa, lru_ba, lru_wx, lru_bx, lru_lambda, q_norm_g, k_norm_g, proj_rnn, proj_attn, w_out, ln2_g, w_up, ffn_conv_w, ffn_conv_b, w_down):
    x2 = x.reshape(M_TOKENS, D_MODEL)
    half = HEAD_DIM // 2
    inv_freq = ROPE_THETA ** (-jnp.arange(half, dtype=F32) / half)
    inv2 = jnp.concatenate([inv_freq, inv_freq]).reshape(1, HEAD_DIM)
    cos_t, sin_t = _rope_tables(positions.reshape(M_TOKENS, 1), inv2)

    qk_scale = jnp.concatenate([jnp.full((N_GROUPS, 1, SLAB), HEAD_DIM ** -0.5, F32),
                                jnp.ones((N_GROUPS, 1, SLAB), F32)], axis=0)
    row = lambda v: v.reshape(1, -1)
    for l in range(DEPTH):
        gains = jnp.concatenate([q_norm_g[l].reshape(N_GROUPS, 1, SLAB),
                                 k_norm_g[l].reshape(N_GROUPS, 1, SLAB)], axis=0)
        qkg = jnp.concatenate([gains, qk_scale], axis=1)
        z = _in_proj(x2, row(ln1_g[l]), w_in[l].astype(BF16), cos_t, sin_t, qkg)
        y_rnn = _rnn(z, rnn_conv_w[l], row(rnn_conv_b[l]), lru_wa[l].astype(BF16), row(lru_ba[l]),
                     lru_wx[l].astype(BF16), row(lru_bx[l]), row(lru_lambda[l]))
        y_att = _attn(z)
        x2, h2 = _merge_out(y_rnn, y_att, z, x2, proj_rnn[l].astype(BF16), proj_attn[l].astype(BF16),
                            w_out[l].astype(BF16), row(ln2_g[l]))
        x2 = _ffn(h2, x2, w_up[l].astype(BF16), ffn_conv_w[l], row(ffn_conv_b[l]),
                  w_down[l].astype(BF16))
    return x2.reshape(BATCH, SEQ, D_MODEL)
```

```python
import functools

import jax
import jax.numpy as jnp
from jax import lax
from jax.experimental import pallas as pl
from jax.experimental.pallas import tpu as pltpu

D_MODEL = 1024
BATCH = 8
SEQ = 2048
DEPTH = 4
M_TOKENS = BATCH * SEQ

D_RNN = D_MODEL
N_LRU_BLOCKS = 8
LRU_BLOCK = D_RNN // N_LRU_BLOCKS
RNN_CONV_WIDTH = 4
LRU_C = 8.0
HEAD_DIM = 128
HEADS_PER_GROUP = 4
DILATED_GROUPS = ((128, 1), (512, 4), (2048, 16))
N_GROUPS = len(DILATED_GROUPS)
N_ATT_HEADS = N_GROUPS * HEADS_PER_GROUP
ATT_WIDTH = N_ATT_HEADS * HEAD_DIM
ATT_OUT_WIDTH = HEADS_PER_GROUP * HEAD_DIM
BLOCK = 128
SPAN = 128
ROPE_THETA = 10000.0
D_FF = 3 * D_MODEL
FFN_CONV_WIDTH = 3
EPS = 1e-6
NEG_INF = -1e30
N_IN = 2 * D_RNN + 3 * ATT_WIDTH + 2 * D_MODEL

F32 = jnp.float32
BF16 = jnp.bfloat16

SUBLANES = 8
BF16_ROWS = 16
SLAB = 512
VMEM_LIMIT = 56 * 1024 * 1024


def _rms(x):
    return x * lax.rsqrt(jnp.mean(x * x, axis=-1, keepdims=True) + EPS)


SUBSTREAMS = 16
SUB_LEN = SEQ // SUBSTREAMS
IN_ROWS = 512
IN_STEPS = SEQ // IN_ROWS
SUB_ROWS = IN_ROWS // SUBSTREAMS


def _rope_kernel(pos_ref, inv_ref, cos_ref, sin_ref, cos_scr, sin_scr):
    lane = lax.broadcasted_iota(jnp.int32, (1, HEAD_DIM), 1)
    sign = jnp.where(lane < HEAD_DIM // 2, -1.0, 1.0).astype(F32)
    rows = 256

    def body(c, carry):
        r0 = pl.multiple_of(c * rows, rows)
        ang = pos_ref[pl.ds(r0, rows), :].astype(F32) * inv_ref[...]
        cos_scr[pl.ds(r0, rows), :] = jnp.cos(ang)
        sin_scr[pl.ds(r0, rows), :] = jnp.sin(ang) * sign
        return carry

    lax.fori_loop(0, SEQ // rows, body, 0)
    for c in range(IN_STEPS):
        for j in range(SUBSTREAMS):
            src = pl.ds(c * IN_ROWS + j, SUB_ROWS, stride=SUBSTREAMS)
            cos_ref[0, j, c] = cos_scr[src, :]
            sin_ref[0, j, c] = sin_scr[src, :]


def _rope_tables(pos, inv2):
    shape = (BATCH, SUBSTREAMS, IN_STEPS, SUB_ROWS, HEAD_DIM)
    out = jax.ShapeDtypeStruct(shape, F32)
    spec = pl.BlockSpec((1,) + shape[1:], lambda b: (b, 0, 0, 0, 0))
    return pl.pallas_call(
        _rope_kernel,
        grid=(BATCH,),
        in_specs=[pl.BlockSpec((SEQ, 1), lambda b: (b, 0)),
                  pl.BlockSpec((1, HEAD_DIM), lambda b: (0, 0))],
        out_specs=[spec, spec],
        out_shape=[out, out],
        scratch_shapes=[pltpu.VMEM((SEQ, HEAD_DIM), F32), pltpu.VMEM((SEQ, HEAD_DIM), F32)],
        compiler_params=pltpu.CompilerParams(dimension_semantics=("parallel",)),
        name="rope_tables",
    )(pos, inv2)


NAT_COL_BLOCKS = (0, 1, 2, 3, 13, 14, 15, 16)
SLAB_XR, SLAB_GR, SLAB_GRNN, SLAB_GATT = 0, 2, 4, 6
Q_COL0, K_COL0, V_COL0 = 4, 7, 10
ACC_SLOTS = 3


def _in_proj_kernel(x_ref, ln_ref, w_ref, cos_ref, sin_ref, qg_ref, kg_ref, zn_ref, zp_ref, h_scr, acc_scr):
    def norm(q, carry):
        rows = pl.ds(pl.multiple_of(q * BLOCK, BLOCK), BLOCK)
        h_scr[rows, :] = (_rms(x_ref[rows, :]) * ln_ref[...]).astype(BF16)
        return carry

    lax.fori_loop(0, IN_ROWS // BLOCK, norm, 0)
    heads = [slice(hh * HEAD_DIM, (hh + 1) * HEAD_DIM) for hh in range(HEADS_PER_GROUP)]

    def project(col_block, slot):
        acc = jnp.dot(h_scr[...], w_ref[:, col_block * SLAB:(col_block + 1) * SLAB],
                      preferred_element_type=F32)
        for hh, cols in enumerate(heads):
            acc_scr[slot, hh] = acc[:, cols]

    def store_natural(slab, slot):
        for hh, cols in enumerate(heads):
            zn_ref[slab, :, cols] = acc_scr[slot, hh].astype(BF16)

    def store_substreams(slab, gain_ref, scale, group, slot):
        subs = range(SUBSTREAMS)
        for hh, cols in enumerate(heads):
            vals = [acc_scr[slot, hh, pl.ds(j, SUB_ROWS, stride=SUBSTREAMS), :] for j in subs]
            if gain_ref is not None:
                head = group * HEADS_PER_GROUP + hh
                gain = gain_ref[:, head * HEAD_DIM:(head + 1) * HEAD_DIM]
                ms = [jnp.mean(v * v, axis=-1, keepdims=True) for v in vals]
                inv = [lax.rsqrt(m + EPS) for m in ms]
                ys = [v * i * gain for v, i in zip(vals, inv)]
                rolled = [pltpu.roll(y, HEAD_DIM // 2, axis=1) for y in ys]
                vals = [y * cos_ref[0, j, 0] + r * sin_ref[0, j, 0] for j, y, r in zip(subs, ys, rolled)]
                if scale is not None:
                    vals = [v * scale for v in vals]
            for j, v in zip(subs, vals):
                zp_ref[slab, 0, j, 0, :, cols] = v.astype(BF16)

    natural = [(cb, functools.partial(store_natural, s)) for s, cb in enumerate(NAT_COL_BLOCKS)]
    dilated = []
    for g in range(N_GROUPS):
        dilated.append((Q_COL0 + g, functools.partial(store_substreams, 3 * g, qg_ref, HEAD_DIM ** -0.5, g)))
        dilated.append((K_COL0 + g, functools.partial(store_substreams, 3 * g + 1, kg_ref, None, g)))
    for g in range(N_GROUPS):
        dilated.append((V_COL0 + g, functools.partial(store_substreams, 3 * g + 2, None, None, g)))
    jobs = []
    for idx, job in enumerate(dilated):
        jobs.append(job)
        if idx < len(natural):
            jobs.append(natural[idx])
    ahead = ACC_SLOTS - 1
    for idx in range(min(ahead, len(jobs))):
        project(jobs[idx][0], idx % ACC_SLOTS)
    for idx, (_, finish) in enumerate(jobs):
        if idx + ahead < len(jobs):
            project(jobs[idx + ahead][0], (idx + ahead) % ACC_SLOTS)
        finish(idx % ACC_SLOTS)


def _in_proj(x2, ln_g, w_bf, cos_t, sin_t, q_gain, k_gain):
    const = lambda shape: pl.BlockSpec(shape, lambda b, c: (0,) * len(shape))
    table = pl.BlockSpec((1, SUBSTREAMS, 1, SUB_ROWS, HEAD_DIM), lambda b, c: (b, 0, c, 0, 0))
    zn, zp = pl.pallas_call(
        _in_proj_kernel,
        grid=(BATCH, IN_STEPS),
        in_specs=[
            pl.BlockSpec((IN_ROWS, D_MODEL), lambda b, c: (b * IN_STEPS + c, 0)),
            const((1, D_MODEL)),
            pl.BlockSpec((D_MODEL, N_IN), lambda b, c: (0, 0), pipeline_mode=pl.Buffered(1)),
            table, table,
            const((1, ATT_WIDTH)),
            const((1, ATT_WIDTH)),
        ],
        out_specs=[
            pl.BlockSpec((len(NAT_COL_BLOCKS), IN_ROWS, SLAB), lambda b, c: (0, b * IN_STEPS + c, 0)),
            pl.BlockSpec((3 * N_GROUPS, 1, SUBSTREAMS, 1, SUB_ROWS, SLAB), lambda b, c: (0, b, 0, c, 0, 0)),
        ],
        out_shape=[
            jax.ShapeDtypeStruct((len(NAT_COL_BLOCKS), M_TOKENS, SLAB), BF16),
            jax.ShapeDtypeStruct((3 * N_GROUPS, BATCH, SUBSTREAMS, IN_STEPS, SUB_ROWS, SLAB), BF16),
        ],
        scratch_shapes=[pltpu.VMEM((IN_ROWS, D_MODEL), BF16),
                        pltpu.VMEM((ACC_SLOTS, HEADS_PER_GROUP, IN_ROWS, HEAD_DIM), F32)],
        compiler_params=pltpu.CompilerParams(
            dimension_semantics=("parallel", "parallel"), vmem_limit_bytes=VMEM_LIMIT),
        name="in_proj",
    )(x2, ln_g, w_bf, cos_t, sin_t, q_gain, k_gain)
    return zn, zp.reshape(3 * N_GROUPS, M_TOKENS, SLAB)


RNN_COLS = 512
RNN_ROWS = 256
RNN_PAD = SUBLANES


def _rnn_kernel(xr_ref, gr_ref, cw_ref, cb_ref, wa_ref, ba_ref, wx_ref, bx_ref, lam_ref,
                o_ref, xs_scr, a_scr, u_scr):
    xs_scr[0:RNN_PAD, :] = jnp.zeros((RNN_PAD, RNN_COLS), F32)

    def stage(c, carry):
        r0 = pl.multiple_of(c * RNN_ROWS, RNN_ROWS)
        xs_scr[pl.ds(RNN_PAD + r0, RNN_ROWS), :] = xr_ref[0, pl.ds(r0, RNN_ROWS), :].astype(F32)
        return carry

    lax.fori_loop(0, SEQ // RNN_ROWS, stage, 0)

    softplus_neg_lam = jax.nn.softplus(-lam_ref[...])

    def gates(c, carry):
        r0 = pl.multiple_of(c * RNN_ROWS, RNN_ROWS)
        win = xs_scr[pl.ds(r0, RNN_ROWS + RNN_PAD), :]
        xc = cb_ref[...]
        for i in range(RNN_CONV_WIDTH):
            back = RNN_CONV_WIDTH - 1 - i
            tap = win if back == 0 else pltpu.roll(win, back, axis=0)
            xc = xc + tap[RNN_PAD:] * cw_ref[i:i + 1, :]
        xc_bf = xc.astype(BF16)
        for nb in range(RNN_COLS // LRU_BLOCK):
            cols = slice(nb * LRU_BLOCK, (nb + 1) * LRU_BLOCK)
            xb = xc_bf[:, cols]
            r = jax.nn.sigmoid(jnp.dot(xb, wa_ref[nb], preferred_element_type=F32) + ba_ref[:, cols])
            ig = jax.nn.sigmoid(jnp.dot(xb, wx_ref[nb], preferred_element_type=F32) + bx_ref[:, cols])
            log_a = -LRU_C * r * softplus_neg_lam[:, cols]
            a = jnp.exp(log_a)
            mult = jnp.sqrt(-jnp.tanh(log_a) * (a * a + 1.0))
            a_scr[pl.ds(r0, RNN_ROWS), cols] = a
            u_scr[pl.ds(r0, RNN_ROWS), cols] = mult * (ig * xc[:, cols])
        return carry

    lax.fori_loop(0, SEQ // RNN_ROWS, gates, 0)

    row = lax.broadcasted_iota(jnp.int32, (SUBLANES, RNN_COLS), 0)

    def scan(t, carry):
        r0 = pl.multiple_of(t * BF16_ROWS, BF16_ROWS)
        a16 = a_scr[pl.ds(r0, BF16_ROWS), :]
        u16 = u_scr[pl.ds(r0, BF16_ROWS), :]
        hs = []
        for half in range(BF16_ROWS // SUBLANES):
            a = a16[half * SUBLANES:(half + 1) * SUBLANES]
            u = u16[half * SUBLANES:(half + 1) * SUBLANES]
            for s in (1, 2, 4):
                a_prev = jnp.where(row >= s, pltpu.roll(a, s, axis=0), 1.0)
                u_prev = jnp.where(row >= s, pltpu.roll(u, s, axis=0), 0.0)
                u = a * u_prev + u
                a = a * a_prev
            h = a * carry + u
            carry = jnp.broadcast_to(h[SUBLANES - 1:SUBLANES, :], (SUBLANES, RNN_COLS))
            hs.append(h)
        gate = jax.nn.gelu(gr_ref[0, pl.ds(r0, BF16_ROWS), :].astype(F32))
        o_ref[pl.ds(r0, BF16_ROWS), :] = (jnp.concatenate(hs, axis=0) * gate).astype(BF16)
        return carry

    lax.fori_loop(0, SEQ // BF16_ROWS, scan, jnp.zeros((SUBLANES, RNN_COLS), F32))


def _rnn(z, conv_w, conv_b, wa_bf, ba, wx_bf, bx, lam):
    n_ct = D_RNN // RNN_COLS
    blocks_per_ct = RNN_COLS // LRU_BLOCK
    vec = pl.BlockSpec((1, RNN_COLS), lambda b, c: (0, c))
    gate_w = pl.BlockSpec((blocks_per_ct, LRU_BLOCK, LRU_BLOCK), lambda b, c: (c, 0, 0))
    return pl.pallas_call(
        _rnn_kernel,
        grid=(BATCH, n_ct),
        in_specs=[
            pl.BlockSpec((1, SEQ, SLAB), lambda b, c: (SLAB_XR + c, b, 0)),
            pl.BlockSpec((1, SEQ, SLAB), lambda b, c: (SLAB_GR + c, b, 0)),
            pl.BlockSpec((RNN_CONV_WIDTH, RNN_COLS), lambda b, c: (0, c)),
            vec, gate_w, vec, gate_w, vec, vec,
        ],
        out_specs=pl.BlockSpec((SEQ, RNN_COLS), lambda b, c: (b, c)),
        out_shape=jax.ShapeDtypeStruct((M_TOKENS, D_RNN), BF16),
        scratch_shapes=[pltpu.VMEM((SEQ + RNN_PAD, RNN_COLS), F32),
                        pltpu.VMEM((SEQ, RNN_COLS), F32),
                        pltpu.VMEM((SEQ, RNN_COLS), F32)],
        compiler_params=pltpu.CompilerParams(
            dimension_semantics=("parallel", "parallel"), vmem_limit_bytes=VMEM_LIMIT),
        name="rnn",
    )(z, z, conv_w, conv_b, wa_bf, ba, wx_bf, bx, lam)


ATT_HEADS = 2
MERGE_ROWS = 256
Q_TILE = {1: 256, 4: 128, 16: 128}


def _tile_geometry(d):
    pieces = SUBSTREAMS // d
    return pieces, Q_TILE[d] // pieces, d


def _attn_masks():
    import numpy as np
    masks = []
    for _, d in DILATED_GROUPS:
        pieces, rows, _ = _tile_geometry(d)
        if pieces == 1:
            pos_q = np.arange(rows)
            first = pos_q[:, None] - pos_q[None, :]
            masks.append((first, None))
            continue
        pos_q = (pieces * np.arange(rows)[None, :] + np.arange(pieces)[:, None]).reshape(-1)
        pos_k2 = (pieces * np.arange(2 * rows)[None, :] + np.arange(pieces)[:, None]).reshape(-1)
        first = pos_q[:, None] - pos_q[None, :]
        band = (pos_q[:, None] + Q_TILE[d]) - pos_k2[None, :]
        masks.append((first, band))
    out = []
    for first, band in masks:
        for dist in (first, band):
            if dist is not None:
                out.append(np.where((dist >= 0) & (dist <= SPAN), 0.0, NEG_INF).astype(np.float32))
    return out


def _attn_kernel(q0, k0, v0, q1, k1, v1, q2, k2, v2, m0f, m0b, m1f, m1b, m2f, o_ref, o_scr, l_scr, y_scr):
    refs = ((q0, k0, v0, m0f, m0b), (q1, k1, v1, m1f, m1b), (q2, k2, v2, m2f, None))

    def tile_rows(d, blk, stream, prev):
        pieces, rows, step = _tile_geometry(d)
        start = pl.multiple_of((blk - prev) * rows, rows)
        return [pl.ds((stream + u * step) * SUB_LEN + start, rows * (1 + prev)) for u in range(pieces)]

    def gather(ref, slices, cols):
        parts = [ref[0, sl, cols] for sl in slices]
        return parts[0] if len(parts) == 1 else jnp.concatenate(parts, axis=0)

    def attend(g, tiles):
        q_ref, k_ref, v_ref, first_ref, band_ref = refs[g]
        d = DILATED_GROUPS[g][1]
        pieces, rows, _ = _tile_geometry(d)
        chains = [(t, hh) for t in tiles for hh in range(ATT_HEADS)]
        col = lambda hh: slice(hh * HEAD_DIM, (hh + 1) * HEAD_DIM)
        s = []
        for (blk, stream, prev), hh in chains:
            q = gather(q_ref, tile_rows(d, blk, stream, 0), col(hh))
            k = gather(k_ref, tile_rows(d, blk, stream, int(prev)), col(hh))
            bias = band_ref[...] if prev else first_ref[...]
            s.append(lax.dot_general(q, k, (((1,), (1,)), ((), ())), preferred_element_type=F32) + bias)
        m = [jnp.max(x, axis=-1, keepdims=True) for x in s]
        p = [jnp.exp(x - mx) for x, mx in zip(s, m)]
        den = [jnp.sum(x, axis=-1, keepdims=True) for x in p]
        out = []
        for ((blk, stream, prev), hh), px in zip(chains, p):
            v = gather(v_ref, tile_rows(d, blk, stream, int(prev)), col(hh))
            out.append(jnp.dot(px.astype(BF16), v, preferred_element_type=F32))
        for ((blk, stream, prev), hh), ox, mx, dx in zip(chains, out, m, den):
            ox = ox / dx
            lse = jnp.broadcast_to(mx + jnp.log(dx), ox.shape)
            for u, sl in enumerate(tile_rows(d, blk, stream, 0)):
                o_scr[g * ATT_HEADS + hh, sl, :] = ox[u * rows:(u + 1) * rows]
                l_scr[g * ATT_HEADS + hh, sl, :] = lse[u * rows:(u + 1) * rows]

    attend(0, [(0, 0, False)])

    def body0(blk, carry):
        attend(0, [(blk, 0, True)])
        return carry

    lax.fori_loop(1, SEQ // Q_TILE[1], body0, 0)

    def body1(stream, carry):
        attend(1, [(0, stream, False)] + [(blk, stream, True) for blk in range(1, SEQ // 4 // Q_TILE[4])])
        return carry

    lax.fori_loop(0, 4, body1, 0)

    def body2(t, carry):
        attend(2, [(0, t * 4 + u, False) for u in range(4)])
        return carry

    lax.fori_loop(0, SUBSTREAMS // 4, body2, 0)

    def merge(c, carry):
        rows = pl.ds(pl.multiple_of(c * MERGE_ROWS, MERGE_ROWS), MERGE_ROWS)
        for hh in range(ATT_HEADS):
            l = [l_scr[g * ATT_HEADS + hh, rows, :] for g in range(N_GROUPS)]
            o = [o_scr[g * ATT_HEADS + hh, rows, :] for g in range(N_GROUPS)]
            top = jnp.maximum(jnp.maximum(l[0], l[1]), l[2])
            e = [jnp.exp(lg - top) for lg in l]
            y_scr[hh, rows, :] = (e[0] * o[0] + e[1] * o[1] + e[2] * o[2]) / (e[0] + e[1] + e[2])
        return carry

    lax.fori_loop(0, SEQ // MERGE_ROWS, merge, 0)

    def unpermute(t, carry):
        for u in range(8):
            m = t * 8 + u
            dst = pl.ds(pl.multiple_of(m * SUBSTREAMS, SUBSTREAMS), SUBSTREAMS)
            for hh in range(ATT_HEADS):
                o_ref[dst, hh * HEAD_DIM:(hh + 1) * HEAD_DIM] = (
                    y_scr[hh, pl.ds(m, SUBSTREAMS, stride=SUB_LEN), :].astype(BF16))
        return carry

    lax.fori_loop(0, SUB_LEN // 8, unpermute, 0)


def _attn(zp):
    width = ATT_HEADS * HEAD_DIM
    specs = []
    for g in range(N_GROUPS):
        for t in range(3):
            specs.append(pl.BlockSpec((1, SEQ, width), lambda b, h, s=3 * g + t: (s, b, h)))
    masks = [jnp.asarray(mk) for mk in _attn_masks()]
    specs += [pl.BlockSpec(mk.shape, lambda b, h: (0, 0)) for mk in masks]
    planes = N_GROUPS * ATT_HEADS
    return pl.pallas_call(
        _attn_kernel,
        grid=(BATCH, HEADS_PER_GROUP // ATT_HEADS),
        in_specs=specs,
        out_specs=pl.BlockSpec((SEQ, width), lambda b, h: (b, h)),
        out_shape=jax.ShapeDtypeStruct((M_TOKENS, ATT_OUT_WIDTH), BF16),
        scratch_shapes=[pltpu.VMEM((planes, SEQ, HEAD_DIM), F32),
                        pltpu.VMEM((planes, SEQ, HEAD_DIM), F32),
                        pltpu.VMEM((ATT_HEADS, SEQ, HEAD_DIM), F32)],
        compiler_params=pltpu.CompilerParams(
            dimension_semantics=("parallel", "parallel"), vmem_limit_bytes=VMEM_LIMIT),
        name="attn",
    )(*([zp] * 9), *masks)


MO_ROWS = 512


def _merge_out_kernel(yr_ref, ya_ref, grnn_ref, gatt_ref, x_ref, pr_ref, pa_ref, wo_ref, ln2_ref,
                      xo_ref, h2_ref):
    t_rnn = jnp.dot(yr_ref[...], pr_ref[...], preferred_element_type=F32)
    t_att = jnp.dot(ya_ref[...], pa_ref[...], preferred_element_type=F32)
    halves = []
    for hf in range(D_MODEL // SLAB):
        cols = slice(hf * SLAB, (hf + 1) * SLAB)
        g_rnn = jax.nn.sigmoid(grnn_ref[hf].astype(F32))
        g_att = jax.nn.sigmoid(gatt_ref[hf].astype(F32))
        halves.append((g_rnn * t_rnn[:, cols] + g_att * t_att[:, cols]).astype(BF16))
    merged = jnp.concatenate(halves, axis=1)
    xn = x_ref[...] + jnp.dot(merged, wo_ref[...], preferred_element_type=F32)
    xo_ref[...] = xn
    h2_ref[...] = (_rms(xn) * ln2_ref[...]).astype(BF16)


def _merge_out(y_rnn, y_att, z, x2, pr_bf, pa_bf, wo_bf, ln2):
    slabs_per_gate = D_MODEL // SLAB
    full = lambda shape: pl.BlockSpec(shape, lambda i: (0, 0))
    return pl.pallas_call(
        _merge_out_kernel,
        grid=(M_TOKENS // MO_ROWS,),
        in_specs=[
            pl.BlockSpec((MO_ROWS, D_RNN), lambda i: (i, 0)),
            pl.BlockSpec((MO_ROWS, ATT_OUT_WIDTH), lambda i: (i, 0)),
            pl.BlockSpec((slabs_per_gate, MO_ROWS, SLAB), lambda i: (SLAB_GRNN // slabs_per_gate, i, 0)),
            pl.BlockSpec((slabs_per_gate, MO_ROWS, SLAB), lambda i: (SLAB_GATT // slabs_per_gate, i, 0)),
            pl.BlockSpec((MO_ROWS, D_MODEL), lambda i: (i, 0)),
            full((D_RNN, D_MODEL)), full((ATT_OUT_WIDTH, D_MODEL)), full((D_MODEL, D_MODEL)),
            full((1, D_MODEL)),
        ],
        out_specs=[pl.BlockSpec((MO_ROWS, D_MODEL), lambda i: (i, 0)),
                   pl.BlockSpec((MO_ROWS, D_MODEL), lambda i: (i, 0))],
        out_shape=[jax.ShapeDtypeStruct((M_TOKENS, D_MODEL), F32),
                   jax.ShapeDtypeStruct((M_TOKENS, D_MODEL), BF16)],
        compiler_params=pltpu.CompilerParams(
            dimension_semantics=("parallel",), vmem_limit_bytes=VMEM_LIMIT),
        name="merge_out",
    )(y_rnn, y_att, z, z, x2, pr_bf, pa_bf, wo_bf, ln2)


FFN_ROWS = 1024
FFN_COLS = 512
FFN_CHUNK = 512
FFN_HALO = BF16_ROWS


def _ffn_kernel(h2_ref, halo_ref, wup_ref, cw_ref, cb_ref, wd_ref, x_ref, o_ref, hbuf):
    seq_start = pl.program_id(0) % (SEQ // FFN_ROWS) == 0

    @pl.when(seq_start)
    def _():
        hbuf[0:FFN_HALO, :] = jnp.zeros((FFN_HALO, D_MODEL), BF16)

    @pl.when(jnp.logical_not(seq_start))
    def _():
        hbuf[0:FFN_HALO, :] = halo_ref[...]

    hbuf[FFN_HALO:, :] = h2_ref[...]
    n_k = D_FF // FFN_COLS

    def up(hb, k):
        cols = [slice(base + k * FFN_COLS, base + (k + 1) * FFN_COLS) for base in (0, D_FF)]
        return [(jnp.dot(hb, wup_ref[:, c], preferred_element_type=F32), c) for c in cols]

    def conv(u, cols):
        out = cb_ref[:, cols]
        for t in range(FFN_CONV_WIDTH):
            back = FFN_CONV_WIDTH - 1 - t
            tap = u if back == 0 else pltpu.roll(u, back, axis=0)
            out = out + tap[FFN_HALO:] * cw_ref[t:t + 1, cols]
        return out

    def chunk(c, carry):
        r0 = pl.multiple_of(c * FFN_CHUNK, FFN_CHUNK)
        hb = hbuf[pl.ds(r0, FFN_CHUNK + FFN_HALO), :]
        acc = x_ref[pl.ds(r0, FFN_CHUNK), :]
        pre = up(hb, 0)
        for k in range(n_k):
            nxt = up(hb, k + 1) if k + 1 < n_k else None
            (ug, gcols), (uv, vcols) = pre
            act = (jax.nn.gelu(conv(ug, gcols)) * conv(uv, vcols)).astype(BF16)
            acc = acc + jnp.dot(act, wd_ref[k * FFN_COLS:(k + 1) * FFN_COLS, :], preferred_element_type=F32)
            pre = nxt
        o_ref[pl.ds(r0, FFN_CHUNK), :] = acc
        return carry

    lax.fori_loop(0, FFN_ROWS // FFN_CHUNK, chunk, 0)


def _ffn(h2, x2, wup_bf, conv_w, conv_b, wd_bf):
    halo_blocks = FFN_ROWS // FFN_HALO
    resident = lambda shape: pl.BlockSpec(shape, lambda i: (0, 0), pipeline_mode=pl.Buffered(1))
    return pl.pallas_call(
        _ffn_kernel,
        grid=(M_TOKENS // FFN_ROWS,),
        in_specs=[
            pl.BlockSpec((FFN_ROWS, D_MODEL), lambda i: (i, 0)),
            pl.BlockSpec((FFN_HALO, D_MODEL), lambda i: (jnp.maximum(i * halo_blocks - 1, 0), 0)),
            resident((D_MODEL, 2 * D_FF)),
            resident((FFN_CONV_WIDTH, 2 * D_FF)),
            resident((1, 2 * D_FF)),
            resident((D_FF, D_MODEL)),
            pl.BlockSpec((FFN_ROWS, D_MODEL), lambda i: (i, 0)),
        ],
        out_specs=pl.BlockSpec((FFN_ROWS, D_MODEL), lambda i: (i, 0)),
        out_shape=jax.ShapeDtypeStruct((M_TOKENS, D_MODEL), F32),
        scratch_shapes=[pltpu.VMEM((FFN_ROWS + FFN_HALO, D_MODEL), BF16)],
        compiler_params=pltpu.CompilerParams(
            dimension_semantics=("parallel",), vmem_limit_bytes=VMEM_LIMIT),
        name="ffn",
    )(h2, h2, wup_bf, conv_w, conv_b, wd_bf, x2)


def kernel(x, positions, ln1_g, w_in, rnn_conv_w, rnn_conv_b, lru_wa, lru_ba, lru_wx, lru_bx, lru_lambda, q_norm_g, k_norm_g, proj_rnn, proj_attn, w_out, ln2_g, w_up, ffn_conv_w, ffn_conv_b, w_down):
    x2 = x.reshape(M_TOKENS, D_MODEL)
    half = HEAD_DIM // 2
    inv_freq = ROPE_THETA ** (-jnp.arange(half, dtype=F32) / half)
    inv2 = jnp.concatenate([inv_freq, inv_freq]).reshape(1, HEAD_DIM)
    cos_t, sin_t = _rope_tables(positions.reshape(M_TOKENS, 1), inv2)

    row = lambda v: v.reshape(1, -1)
    for l in range(DEPTH):
        zn, zp = _in_proj(x2, row(ln1_g[l]), w_in[l].astype(BF16), cos_t, sin_t,
                          row(q_norm_g[l]), row(k_norm_g[l]))
        y_rnn = _rnn(zn, rnn_conv_w[l], row(rnn_conv_b[l]), lru_wa[l].astype(BF16), row(lru_ba[l]),
                     lru_wx[l].astype(BF16), row(lru_bx[l]), row(lru_lambda[l]))
        y_att = _attn(zp)
        x2, h2 = _merge_out(y_rnn, y_att, zn, x2, proj_rnn[l].astype(BF16), proj_attn[l].astype(BF16),
                            w_out[l].astype(BF16), row(ln2_g[l]))
        x2 = _ffn(h2, x2, w_up[l].astype(BF16), ffn_conv_w[l], row(ffn_conv_b[l]),
                  w_down[l].astype(BF16))
    return x2.reshape(BATCH, SEQ, D_MODEL)
```

```python
import functools

import jax
import jax.numpy as jnp
from jax import lax
from jax.experimental import pallas as pl
from jax.experimental.pallas import tpu as pltpu

D_MODEL = 1024
BATCH = 8
SEQ = 2048
DEPTH = 4
M_TOKENS = BATCH * SEQ

D_RNN = D_MODEL
N_LRU_BLOCKS = 8
LRU_BLOCK = D_RNN // N_LRU_BLOCKS
RNN_CONV_WIDTH = 4
LRU_C = 8.0
HEAD_DIM = 128
HEADS_PER_GROUP = 4
DILATED_GROUPS = ((128, 1), (512, 4), (2048, 16))
N_GROUPS = len(DILATED_GROUPS)
N_ATT_HEADS = N_GROUPS * HEADS_PER_GROUP
ATT_WIDTH = N_ATT_HEADS * HEAD_DIM
ATT_OUT_WIDTH = HEADS_PER_GROUP * HEAD_DIM
SPAN = 128
ROPE_THETA = 10000.0
D_FF = 3 * D_MODEL
FFN_CONV_WIDTH = 3
EPS = 1e-6
NEG_INF = -1e30
N_IN = 2 * D_RNN + 3 * ATT_WIDTH + 2 * D_MODEL

F32 = jnp.float32
BF16 = jnp.bfloat16

SUBLANES = 8
BF16_ROWS = 16
SLAB = 512
VMEM_LIMIT = 56 * 1024 * 1024


def _rms(x):
    return x * lax.rsqrt(jnp.mean(x * x, axis=-1, keepdims=True) + EPS)


SUBSTREAMS = 16
SUB_LEN = SEQ // SUBSTREAMS
IN_ROWS = 512
IN_STEPS = SEQ // IN_ROWS
SUB_ROWS = IN_ROWS // SUBSTREAMS


def _rope_kernel(pos_ref, inv_ref, cos_ref, sin_ref, cos_scr, sin_scr):
    lane = lax.broadcasted_iota(jnp.int32, (1, HEAD_DIM), 1)
    sign = jnp.where(lane < HEAD_DIM // 2, -1.0, 1.0).astype(F32)
    rows = 256

    def body(c, carry):
        r0 = pl.multiple_of(c * rows, rows)
        ang = pos_ref[pl.ds(r0, rows), :].astype(F32) * inv_ref[...]
        cos_scr[pl.ds(r0, rows), :] = jnp.cos(ang)
        sin_scr[pl.ds(r0, rows), :] = jnp.sin(ang) * sign
        return carry

    lax.fori_loop(0, SEQ // rows, body, 0)
    for c in range(IN_STEPS):
        for j in range(SUBSTREAMS):
            src = pl.ds(c * IN_ROWS + j, SUB_ROWS, stride=SUBSTREAMS)
            cos_ref[0, j, c] = cos_scr[src, :]
            sin_ref[0, j, c] = sin_scr[src, :]


def _rope_tables(pos, inv2):
    shape = (BATCH, SUBSTREAMS, IN_STEPS, SUB_ROWS, HEAD_DIM)
    out = jax.ShapeDtypeStruct(shape, F32)
    spec = pl.BlockSpec((1,) + shape[1:], lambda b: (b, 0, 0, 0, 0))
    return pl.pallas_call(
        _rope_kernel,
        grid=(BATCH,),
        in_specs=[pl.BlockSpec((SEQ, 1), lambda b: (b, 0)),
                  pl.BlockSpec((1, HEAD_DIM), lambda b: (0, 0))],
        out_specs=[spec, spec],
        out_shape=[out, out],
        scratch_shapes=[pltpu.VMEM((SEQ, HEAD_DIM), F32), pltpu.VMEM((SEQ, HEAD_DIM), F32)],
        compiler_params=pltpu.CompilerParams(dimension_semantics=("parallel",)),
        name="rope_tables",
    )(pos, inv2)


NAT_COL_BLOCKS = (0, 1, 2, 3, 13, 14, 15, 16)
SLAB_XR, SLAB_GR, SLAB_GRNN, SLAB_GATT = 0, 2, 4, 6
Q_COL0, K_COL0, V_COL0 = 4, 7, 10
ACC_SLOTS = 3


def _in_proj_kernel(h_ref, w_ref, cos_ref, sin_ref, qg_ref, kg_ref, zn_ref, zp_ref, acc_scr):
    heads = [slice(hh * HEAD_DIM, (hh + 1) * HEAD_DIM) for hh in range(HEADS_PER_GROUP)]

    def project(col_block, slot):
        acc = jnp.dot(h_ref[...], w_ref[:, col_block * SLAB:(col_block + 1) * SLAB],
                      preferred_element_type=F32)
        for hh, cols in enumerate(heads):
            acc_scr[slot, hh] = acc[:, cols]

    def store_natural(slab, slot):
        activate = SLAB_GR <= slab < SLAB_GRNN
        for hh, cols in enumerate(heads):
            val = acc_scr[slot, hh]
            zn_ref[slab, :, cols] = (jax.nn.gelu(val) if activate else val).astype(BF16)

    def store_substreams(slab, gain_ref, scale, group, slot):
        subs = range(SUBSTREAMS)
        for hh, cols in enumerate(heads):
            vals = [acc_scr[slot, hh, pl.ds(j, SUB_ROWS, stride=SUBSTREAMS), :] for j in subs]
            if gain_ref is not None:
                head = group * HEADS_PER_GROUP + hh
                gain = gain_ref[:, head * HEAD_DIM:(head + 1) * HEAD_DIM]
                ms = [jnp.mean(v * v, axis=-1, keepdims=True) for v in vals]
                inv = [lax.rsqrt(m + EPS) for m in ms]
                ys = [v * i * gain for v, i in zip(vals, inv)]
                rolled = [pltpu.roll(y, HEAD_DIM // 2, axis=1) for y in ys]
                vals = [y * cos_ref[0, j, 0] + r * sin_ref[0, j, 0] for j, y, r in zip(subs, ys, rolled)]
                if scale is not None:
                    vals = [v * scale for v in vals]
            for j, v in zip(subs, vals):
                zp_ref[slab, 0, j, 0, :, cols] = v.astype(BF16)

    natural = [(cb, functools.partial(store_natural, s)) for s, cb in enumerate(NAT_COL_BLOCKS)]
    dilated = []
    for g in range(N_GROUPS):
        dilated.append((Q_COL0 + g, functools.partial(store_substreams, 3 * g, qg_ref, HEAD_DIM ** -0.5, g)))
        dilated.append((K_COL0 + g, functools.partial(store_substreams, 3 * g + 1, kg_ref, None, g)))
    for g in range(N_GROUPS):
        dilated.append((V_COL0 + g, functools.partial(store_substreams, 3 * g + 2, None, None, g)))
    jobs = []
    for idx, job in enumerate(dilated):
        jobs.append(job)
        if idx < len(natural):
            jobs.append(natural[idx])
    ahead = ACC_SLOTS - 1
    for idx in range(min(ahead, len(jobs))):
        project(jobs[idx][0], idx % ACC_SLOTS)
    for idx, (_, finish) in enumerate(jobs):
        if idx + ahead < len(jobs):
            project(jobs[idx + ahead][0], (idx + ahead) % ACC_SLOTS)
        finish(idx % ACC_SLOTS)


def _in_proj(layer, h, w_bf, cos_t, sin_t, q_gain, k_gain):
    gain = pl.BlockSpec((None, 1, ATT_WIDTH), lambda b, c: (layer, 0, 0))
    table = pl.BlockSpec((1, SUBSTREAMS, 1, SUB_ROWS, HEAD_DIM), lambda b, c: (b, 0, c, 0, 0))
    zn, zp = pl.pallas_call(
        _in_proj_kernel,
        grid=(BATCH, IN_STEPS),
        in_specs=[
            pl.BlockSpec((IN_ROWS, D_MODEL), lambda b, c: (b * IN_STEPS + c, 0)),
            pl.BlockSpec((None, D_MODEL, N_IN), lambda b, c: (layer, 0, 0), pipeline_mode=pl.Buffered(1)),
            table, table, gain, gain,
        ],
        out_specs=[
            pl.BlockSpec((len(NAT_COL_BLOCKS), IN_ROWS, SLAB), lambda b, c: (0, b * IN_STEPS + c, 0)),
            pl.BlockSpec((3 * N_GROUPS, 1, SUBSTREAMS, 1, SUB_ROWS, SLAB), lambda b, c: (0, b, 0, c, 0, 0)),
        ],
        out_shape=[
            jax.ShapeDtypeStruct((len(NAT_COL_BLOCKS), M_TOKENS, SLAB), BF16),
            jax.ShapeDtypeStruct((3 * N_GROUPS, BATCH, SUBSTREAMS, IN_STEPS, SUB_ROWS, SLAB), BF16),
        ],
        scratch_shapes=[pltpu.VMEM((ACC_SLOTS, HEADS_PER_GROUP, IN_ROWS, HEAD_DIM), F32)],
        compiler_params=pltpu.CompilerParams(
            dimension_semantics=("parallel", "parallel"), vmem_limit_bytes=VMEM_LIMIT),
        name="in_proj",
    )(h, w_bf, cos_t, sin_t, q_gain, k_gain)
    return zn, zp.reshape(3 * N_GROUPS, M_TOKENS, SLAB)


RNN_COLS = 512
RNN_ROWS = 256
RNN_PAD = SUBLANES


def _rnn_kernel(xr_ref, gr_ref, cw_ref, cb_ref, wa_ref, ba_ref, wx_ref, bx_ref, lam_ref,
                o_ref, xs_scr, a_scr, u_scr):
    xs_scr[0:RNN_PAD, :] = jnp.zeros((RNN_PAD, RNN_COLS), F32)

    def stage(c, carry):
        r0 = pl.multiple_of(c * RNN_ROWS, RNN_ROWS)
        xs_scr[pl.ds(RNN_PAD + r0, RNN_ROWS), :] = xr_ref[0, pl.ds(r0, RNN_ROWS), :].astype(F32)
        return carry

    lax.fori_loop(0, SEQ // RNN_ROWS, stage, 0)

    softplus_neg_lam = jax.nn.softplus(-lam_ref[...])

    def gates(c, carry):
        r0 = pl.multiple_of(c * RNN_ROWS, RNN_ROWS)
        win = xs_scr[pl.ds(r0, RNN_ROWS + RNN_PAD), :]
        xc = cb_ref[...]
        for i in range(RNN_CONV_WIDTH):
            back = RNN_CONV_WIDTH - 1 - i
            tap = win if back == 0 else pltpu.roll(win, back, axis=0)
            xc = xc + tap[RNN_PAD:] * cw_ref[i:i + 1, :]
        xc_bf = xc.astype(BF16)
        for nb in range(RNN_COLS // LRU_BLOCK):
            cols = slice(nb * LRU_BLOCK, (nb + 1) * LRU_BLOCK)
            xb = xc_bf[:, cols]
            r = jax.nn.sigmoid(jnp.dot(xb, wa_ref[nb], preferred_element_type=F32) + ba_ref[:, cols])
            ig = jax.nn.sigmoid(jnp.dot(xb, wx_ref[nb], preferred_element_type=F32) + bx_ref[:, cols])
            log_a = -LRU_C * r * softplus_neg_lam[:, cols]
            a = jnp.exp(log_a)
            mult = jnp.sqrt(-jnp.tanh(log_a) * (a * a + 1.0))
            a_scr[pl.ds(r0, RNN_ROWS), cols] = a
            u_scr[pl.ds(r0, RNN_ROWS), cols] = mult * (ig * xc[:, cols])
        return carry

    lax.fori_loop(0, SEQ // RNN_ROWS, gates, 0)

    row = lax.broadcasted_iota(jnp.int32, (SUBLANES, RNN_COLS), 0)

    def scan(t, carry):
        r0 = pl.multiple_of(t * BF16_ROWS, BF16_ROWS)
        a16 = a_scr[pl.ds(r0, BF16_ROWS), :]
        u16 = u_scr[pl.ds(r0, BF16_ROWS), :]
        hs = []
        for half in range(BF16_ROWS // SUBLANES):
            a = a16[half * SUBLANES:(half + 1) * SUBLANES]
            u = u16[half * SUBLANES:(half + 1) * SUBLANES]
            for s in (1, 2, 4):
                a_prev = jnp.where(row >= s, pltpu.roll(a, s, axis=0), 1.0)
                u_prev = jnp.where(row >= s, pltpu.roll(u, s, axis=0), 0.0)
                u = a * u_prev + u
                a = a * a_prev
            h = a * carry + u
            carry = jnp.broadcast_to(h[SUBLANES - 1:SUBLANES, :], (SUBLANES, RNN_COLS))
            hs.append(h)
        gate = gr_ref[0, pl.ds(r0, BF16_ROWS), :].astype(F32)
        o_ref[pl.ds(r0, BF16_ROWS), :] = (jnp.concatenate(hs, axis=0) * gate).astype(BF16)
        return carry

    lax.fori_loop(0, SEQ // BF16_ROWS, scan, jnp.zeros((SUBLANES, RNN_COLS), F32), unroll=2)


def _rnn(layer, z, conv_w, conv_b, wa_bf, ba, wx_bf, bx, lam):
    n_ct = D_RNN // RNN_COLS
    blocks_per_ct = RNN_COLS // LRU_BLOCK
    vec = pl.BlockSpec((None, 1, RNN_COLS), lambda b, c: (layer, 0, c))
    gate_w = pl.BlockSpec((None, blocks_per_ct, LRU_BLOCK, LRU_BLOCK), lambda b, c: (layer, c, 0, 0))
    return pl.pallas_call(
        _rnn_kernel,
        grid=(BATCH, n_ct),
        in_specs=[
            pl.BlockSpec((1, SEQ, SLAB), lambda b, c: (SLAB_XR + c, b, 0)),
            pl.BlockSpec((1, SEQ, SLAB), lambda b, c: (SLAB_GR + c, b, 0)),
            pl.BlockSpec((None, RNN_CONV_WIDTH, RNN_COLS), lambda b, c: (layer, 0, c)),
            vec, gate_w, vec, gate_w, vec, vec,
        ],
        out_specs=pl.BlockSpec((SEQ, RNN_COLS), lambda b, c: (b, c)),
        out_shape=jax.ShapeDtypeStruct((M_TOKENS, D_RNN), BF16),
        scratch_shapes=[pltpu.VMEM((SEQ + RNN_PAD, RNN_COLS), F32),
                        pltpu.VMEM((SEQ, RNN_COLS), F32),
                        pltpu.VMEM((SEQ, RNN_COLS), F32)],
        compiler_params=pltpu.CompilerParams(
            dimension_semantics=("parallel", "parallel"), vmem_limit_bytes=VMEM_LIMIT),
        name="rnn",
    )(z, z, conv_w, conv_b, wa_bf, ba, wx_bf, bx, lam)


ATT_HEADS = 2
MERGE_ROWS = 256
Q_TILE = {1: 256, 4: 128, 16: 128}


def _tile_geometry(d):
    pieces = SUBSTREAMS // d
    return pieces, Q_TILE[d] // pieces, d


def _attn_masks():
    import numpy as np
    masks = []
    for _, d in DILATED_GROUPS:
        pieces, rows, _ = _tile_geometry(d)
        if pieces == 1:
            pos_q = np.arange(rows)
            first = pos_q[:, None] - pos_q[None, :]
            masks.append((first, None))
            continue
        pos_q = (pieces * np.arange(rows)[None, :] + np.arange(pieces)[:, None]).reshape(-1)
        pos_k2 = (pieces * np.arange(2 * rows)[None, :] + np.arange(pieces)[:, None]).reshape(-1)
        first = pos_q[:, None] - pos_q[None, :]
        band = (pos_q[:, None] + Q_TILE[d]) - pos_k2[None, :]
        masks.append((first, band))
    out = []
    for first, band in masks:
        for dist in (first, band):
            if dist is not None:
                out.append(np.where((dist >= 0) & (dist <= SPAN), 0.0, NEG_INF).astype(np.float32))
    return out


def _attn_kernel(q0, k0, v0, q1, k1, v1, q2, k2, v2, m0f, m0b, m1f, m1b, m2f, o_ref, o_scr, l_scr, y_scr):
    refs = ((q0, k0, v0, m0f, m0b), (q1, k1, v1, m1f, m1b), (q2, k2, v2, m2f, None))

    def tile_rows(d, blk, stream, prev):
        pieces, rows, step = _tile_geometry(d)
        start = pl.multiple_of((blk - prev) * rows, rows)
        return [pl.ds((stream + u * step) * SUB_LEN + start, rows * (1 + prev)) for u in range(pieces)]

    def gather(ref, slices, cols):
        parts = [ref[0, sl, cols] for sl in slices]
        return parts[0] if len(parts) == 1 else jnp.concatenate(parts, axis=0)

    def attend(g, tiles):
        q_ref, k_ref, v_ref, first_ref, band_ref = refs[g]
        d = DILATED_GROUPS[g][1]
        pieces, rows, _ = _tile_geometry(d)
        chains = [(t, hh) for t in tiles for hh in range(ATT_HEADS)]
        col = lambda hh: slice(hh * HEAD_DIM, (hh + 1) * HEAD_DIM)
        s = []
        for (blk, stream, prev), hh in chains:
            q = gather(q_ref, tile_rows(d, blk, stream, 0), col(hh))
            k = gather(k_ref, tile_rows(d, blk, stream, int(prev)), col(hh))
            bias = band_ref[...] if prev else first_ref[...]
            s.append(lax.dot_general(q, k, (((1,), (1,)), ((), ())), preferred_element_type=F32) + bias)
        m = [jnp.max(x, axis=-1, keepdims=True) for x in s]
        p = [jnp.exp(x - mx) for x, mx in zip(s, m)]
        den = [jnp.sum(x, axis=-1, keepdims=True) for x in p]
        out = []
        for ((blk, stream, prev), hh), px in zip(chains, p):
            v = gather(v_ref, tile_rows(d, blk, stream, int(prev)), col(hh))
            out.append(jnp.dot(px.astype(BF16), v, preferred_element_type=F32))
        for ((blk, stream, prev), hh), ox, mx, dx in zip(chains, out, m, den):
            ox = ox / dx
            lse = jnp.broadcast_to(mx + jnp.log(dx), ox.shape)
            for u, sl in enumerate(tile_rows(d, blk, stream, 0)):
                o_scr[g * ATT_HEADS + hh, sl, :] = ox[u * rows:(u + 1) * rows]
                l_scr[g * ATT_HEADS + hh, sl, :] = lse[u * rows:(u + 1) * rows]

    attend(0, [(0, 0, False)])

    def body0(blk, carry):
        attend(0, [(blk, 0, True)])
        return carry

    lax.fori_loop(1, SEQ // Q_TILE[1], body0, 0)

    def body1(stream, carry):
        attend(1, [(0, stream, False)] + [(blk, stream, True) for blk in range(1, SEQ // 4 // Q_TILE[4])])
        return carry

    lax.fori_loop(0, 4, body1, 0)

    def body2(t, carry):
        attend(2, [(0, t * 4 + u, False) for u in range(4)])
        return carry

    lax.fori_loop(0, SUBSTREAMS // 4, body2, 0)

    def merge(c, carry):
        rows = pl.ds(pl.multiple_of(c * MERGE_ROWS, MERGE_ROWS), MERGE_ROWS)
        for hh in range(ATT_HEADS):
            l = [l_scr[g * ATT_HEADS + hh, rows, :] for g in range(N_GROUPS)]
            o = [o_scr[g * ATT_HEADS + hh, rows, :] for g in range(N_GROUPS)]
            top = jnp.maximum(jnp.maximum(l[0], l[1]), l[2])
            e = [jnp.exp(lg - top) for lg in l]
            y_scr[hh, rows, :] = (e[0] * o[0] + e[1] * o[1] + e[2] * o[2]) / (e[0] + e[1] + e[2])
        return carry

    lax.fori_loop(0, SEQ // MERGE_ROWS, merge, 0)

    def unpermute(t, carry):
        for u in range(8):
            m = t * 8 + u
            dst = pl.ds(pl.multiple_of(m * SUBSTREAMS, SUBSTREAMS), SUBSTREAMS)
            for hh in range(ATT_HEADS):
                o_ref[dst, hh * HEAD_DIM:(hh + 1) * HEAD_DIM] = (
                    y_scr[hh, pl.ds(m, SUBSTREAMS, stride=SUB_LEN), :].astype(BF16))
        return carry

    lax.fori_loop(0, SUB_LEN // 8, unpermute, 0)


def _attn(zp):
    width = ATT_HEADS * HEAD_DIM
    specs = []
    for g in range(N_GROUPS):
        for t in range(3):
            specs.append(pl.BlockSpec((1, SEQ, width), lambda b, h, s=3 * g + t: (s, b, h)))
    masks = [jnp.asarray(mk) for mk in _attn_masks()]
    specs += [pl.BlockSpec(mk.shape, lambda b, h: (0, 0)) for mk in masks]
    planes = N_GROUPS * ATT_HEADS
    return pl.pallas_call(
        _attn_kernel,
        grid=(BATCH, HEADS_PER_GROUP // ATT_HEADS),
        in_specs=specs,
        out_specs=pl.BlockSpec((SEQ, width), lambda b, h: (b, h)),
        out_shape=jax.ShapeDtypeStruct((M_TOKENS, ATT_OUT_WIDTH), BF16),
        scratch_shapes=[pltpu.VMEM((planes, SEQ, HEAD_DIM), F32),
                        pltpu.VMEM((planes, SEQ, HEAD_DIM), F32),
                        pltpu.VMEM((ATT_HEADS, SEQ, HEAD_DIM), F32)],
        compiler_params=pltpu.CompilerParams(
            dimension_semantics=("parallel", "parallel"), vmem_limit_bytes=VMEM_LIMIT),
        name="attn",
    )(*([zp] * 9), *masks)


MO_ROWS = 512


def _merge_out_kernel(yr_ref, ya_ref, grnn_ref, gatt_ref, x_ref, pr_ref, pa_ref, wo_ref, ln2_ref,
                      xo_ref, h2_ref):
    t_rnn = jnp.dot(yr_ref[...], pr_ref[...], preferred_element_type=F32)
    t_att = jnp.dot(ya_ref[...], pa_ref[...], preferred_element_type=F32)
    halves = []
    for hf in range(D_MODEL // SLAB):
        cols = slice(hf * SLAB, (hf + 1) * SLAB)
        g_rnn = jax.nn.sigmoid(grnn_ref[hf].astype(F32))
        g_att = jax.nn.sigmoid(gatt_ref[hf].astype(F32))
        halves.append((g_rnn * t_rnn[:, cols] + g_att * t_att[:, cols]).astype(BF16))
    merged = jnp.concatenate(halves, axis=1)
    xn = x_ref[...] + jnp.dot(merged, wo_ref[...], preferred_element_type=F32)
    xo_ref[...] = xn
    h2_ref[...] = (_rms(xn) * ln2_ref[...]).astype(BF16)


def _merge_out(layer, y_rnn, y_att, z, x2, pr_bf, pa_bf, wo_bf, ln2):
    slabs_per_gate = D_MODEL // SLAB
    full = lambda shape: pl.BlockSpec((None,) + shape, lambda i: (layer, 0, 0))
    return pl.pallas_call(
        _merge_out_kernel,
        grid=(M_TOKENS // MO_ROWS,),
        in_specs=[
            pl.BlockSpec((MO_ROWS, D_RNN), lambda i: (i, 0)),
            pl.BlockSpec((MO_ROWS, ATT_OUT_WIDTH), lambda i: (i, 0)),
            pl.BlockSpec((slabs_per_gate, MO_ROWS, SLAB), lambda i: (SLAB_GRNN // slabs_per_gate, i, 0)),
            pl.BlockSpec((slabs_per_gate, MO_ROWS, SLAB), lambda i: (SLAB_GATT // slabs_per_gate, i, 0)),
            pl.BlockSpec((MO_ROWS, D_MODEL), lambda i: (i, 0)),
            full((D_RNN, D_MODEL)), full((ATT_OUT_WIDTH, D_MODEL)), full((D_MODEL, D_MODEL)),
            full((1, D_MODEL)),
        ],
        out_specs=[pl.BlockSpec((MO_ROWS, D_MODEL), lambda i: (i, 0)),
                   pl.BlockSpec((MO_ROWS, D_MODEL), lambda i: (i, 0))],
        out_shape=[jax.ShapeDtypeStruct((M_TOKENS, D_MODEL), F32),
                   jax.ShapeDtypeStruct((M_TOKENS, D_MODEL), BF16)],
        compiler_params=pltpu.CompilerParams(
            dimension_semantics=("parallel",), vmem_limit_bytes=VMEM_LIMIT),
        name="merge_out",
    )(y_rnn, y_att, z, z, x2, pr_bf, pa_bf, wo_bf, ln2)


FFN_ROWS = 1024
FFN_COLS = 512
FFN_CHUNK = 512
FFN_HALO = BF16_ROWS


def _ffn_kernel(emit_h, h2_ref, halo_ref, wup_ref, cw_ref, cb_ref, wd_ref, x_ref, *rest):
    if emit_h:
        ln_ref, o_ref, h_ref, hbuf = rest
    else:
        o_ref, hbuf = rest
    seq_start = pl.program_id(0) % (SEQ // FFN_ROWS) == 0

    @pl.when(seq_start)
    def _():
        hbuf[0:FFN_HALO, :] = jnp.zeros((FFN_HALO, D_MODEL), BF16)

    @pl.when(jnp.logical_not(seq_start))
    def _():
        hbuf[0:FFN_HALO, :] = halo_ref[...]

    hbuf[FFN_HALO:, :] = h2_ref[...]
    n_k = D_FF // FFN_COLS

    def up(hb, k):
        cols = [slice(base + k * FFN_COLS, base + (k + 1) * FFN_COLS) for base in (0, D_FF)]
        return [(jnp.dot(hb, wup_ref[:, c], preferred_element_type=F32), c) for c in cols]

    def conv(u, cols):
        out = cb_ref[:, cols]
        for t in range(FFN_CONV_WIDTH):
            back = FFN_CONV_WIDTH - 1 - t
            tap = u if back == 0 else pltpu.roll(u, back, axis=0)
            out = out + tap[FFN_HALO:] * cw_ref[t:t + 1, cols]
        return out

    def chunk(c, carry):
        r0 = pl.multiple_of(c * FFN_CHUNK, FFN_CHUNK)
        hb = hbuf[pl.ds(r0, FFN_CHUNK + FFN_HALO), :]
        acc = x_ref[pl.ds(r0, FFN_CHUNK), :]
        pre = up(hb, 0)
        for k in range(n_k):
            nxt = up(hb, k + 1) if k + 1 < n_k else None
            (ug, gcols), (uv, vcols) = pre
            act = (jax.nn.gelu(conv(ug, gcols)) * conv(uv, vcols)).astype(BF16)
            acc = acc + jnp.dot(act, wd_ref[k * FFN_COLS:(k + 1) * FFN_COLS, :], preferred_element_type=F32)
            pre = nxt
        o_ref[pl.ds(r0, FFN_CHUNK), :] = acc
        if emit_h:
            h_ref[pl.ds(r0, FFN_CHUNK), :] = (_rms(acc) * ln_ref[...]).astype(BF16)
        return carry

    lax.fori_loop(0, FFN_ROWS // FFN_CHUNK, chunk, 0)


def _ffn(layer, h2, x2, wup_bf, conv_w, conv_b, wd_bf, ln1):
    emit_h = layer + 1 < DEPTH
    halo_blocks = FFN_ROWS // FFN_HALO
    resident = lambda shape: pl.BlockSpec((None,) + shape, lambda i: (layer, 0, 0),
                                          pipeline_mode=pl.Buffered(1))
    tile = pl.BlockSpec((FFN_ROWS, D_MODEL), lambda i: (i, 0))
    in_specs = [
        tile,
        pl.BlockSpec((FFN_HALO, D_MODEL), lambda i: (jnp.maximum(i * halo_blocks - 1, 0), 0)),
        resident((D_MODEL, 2 * D_FF)),
        resident((FFN_CONV_WIDTH, 2 * D_FF)),
        resident((1, 2 * D_FF)),
        resident((D_FF, D_MODEL)),
        tile,
    ]
    args = [h2, h2, wup_bf, conv_w, conv_b, wd_bf, x2]
    out_specs = [tile]
    out_shape = [jax.ShapeDtypeStruct((M_TOKENS, D_MODEL), F32)]
    if emit_h:
        in_specs.append(pl.BlockSpec((None, 1, D_MODEL), lambda i: (layer + 1, 0, 0)))
        args.append(ln1)
        out_specs.append(tile)
        out_shape.append(jax.ShapeDtypeStruct((M_TOKENS, D_MODEL), BF16))
    outs = pl.pallas_call(
        functools.partial(_ffn_kernel, emit_h),
        grid=(M_TOKENS // FFN_ROWS,),
        in_specs=in_specs,
        out_specs=out_specs,
        out_shape=out_shape,
        scratch_shapes=[pltpu.VMEM((FFN_ROWS + FFN_HALO, D_MODEL), BF16)],
        compiler_params=pltpu.CompilerParams(
            dimension_semantics=("parallel",), vmem_limit_bytes=VMEM_LIMIT),
        name="ffn",
    )(*args)
    return (outs[0], outs[1]) if emit_h else (outs[0], None)


def _norm_kernel(x_ref, ln_ref, h_ref):
    h_ref[...] = (_rms(x_ref[...]) * ln_ref[...]).astype(BF16)


def _first_norm(x2, ln1):
    tile = pl.BlockSpec((MO_ROWS, D_MODEL), lambda i: (i, 0))
    return pl.pallas_call(
        _norm_kernel,
        grid=(M_TOKENS // MO_ROWS,),
        in_specs=[tile, pl.BlockSpec((None, 1, D_MODEL), lambda i: (0, 0, 0))],
        out_specs=tile,
        out_shape=jax.ShapeDtypeStruct((M_TOKENS, D_MODEL), BF16),
        compiler_params=pltpu.CompilerParams(dimension_semantics=("parallel",)),
        name="first_norm",
    )(x2, ln1)


def kernel(x, positions, ln1_g, w_in, rnn_conv_w, rnn_conv_b, lru_wa, lru_ba, lru_wx, lru_bx, lru_lambda, q_norm_g, k_norm_g, proj_rnn, proj_attn, w_out, ln2_g, w_up, ffn_conv_w, ffn_conv_b, w_down):
    x2 = x.reshape(M_TOKENS, D_MODEL)
    half = HEAD_DIM // 2
    inv_freq = ROPE_THETA ** (-jnp.arange(half, dtype=F32) / half)
    inv2 = jnp.concatenate([inv_freq, inv_freq]).reshape(1, HEAD_DIM)
    cos_t, sin_t = _rope_tables(positions.reshape(M_TOKENS, 1), inv2)

    rows = lambda v: v.reshape(DEPTH, 1, -1)
    w_in, lru_wa, lru_wx, proj_rnn, proj_attn, w_out, w_up, w_down = (
        w.astype(BF16) for w in (w_in, lru_wa, lru_wx, proj_rnn, proj_attn, w_out, w_up, w_down))
    ln1, ln2, q_gain, k_gain = rows(ln1_g), rows(ln2_g), rows(q_norm_g), rows(k_norm_g)
    conv_b, ba, bx, lam, ffn_b = (rows(v) for v in (rnn_conv_b, lru_ba, lru_bx, lru_lambda, ffn_conv_b))

    h = _first_norm(x2, ln1)
    for l in range(DEPTH):
        zn, zp = _in_proj(l, h, w_in, cos_t, sin_t, q_gain, k_gain)
        y_rnn = _rnn(l, zn, rnn_conv_w, conv_b, lru_wa, ba, lru_wx, bx, lam)
        y_att = _attn(zp)
        x2, h2 = _merge_out(l, y_rnn, y_att, zn, x2, proj_rnn, proj_attn, w_out, ln2)
        x2, h = _ffn(l, h2, x2, w_up, ffn_conv_w, ffn_b, w_down, ln1)
    return x2.reshape(BATCH, SEQ, D_MODEL)
```

```python
import functools

import jax
import jax.numpy as jnp
from jax import lax
from jax.experimental import pallas as pl
from jax.experimental.pallas import tpu as pltpu

D_MODEL = 1024
BATCH = 8
SEQ = 2048
DEPTH = 4
M_TOKENS = BATCH * SEQ

D_RNN = D_MODEL
N_LRU_BLOCKS = 8
LRU_BLOCK = D_RNN // N_LRU_BLOCKS
RNN_CONV_WIDTH = 4
LRU_C = 8.0
HEAD_DIM = 128
HEADS_PER_GROUP = 4
DILATED_GROUPS = ((128, 1), (512, 4), (2048, 16))
N_GROUPS = len(DILATED_GROUPS)
N_ATT_HEADS = N_GROUPS * HEADS_PER_GROUP
ATT_WIDTH = N_ATT_HEADS * HEAD_DIM
ATT_OUT_WIDTH = HEADS_PER_GROUP * HEAD_DIM
SPAN = 128
ROPE_THETA = 10000.0
D_FF = 3 * D_MODEL
FFN_CONV_WIDTH = 3
EPS = 1e-6
NEG_INF = -1e30
N_IN = 2 * D_RNN + 3 * ATT_WIDTH + 2 * D_MODEL

F32 = jnp.float32
BF16 = jnp.bfloat16

SUBLANES = 8
BF16_ROWS = 16
SLAB = 512
VMEM_LIMIT = 56 * 1024 * 1024


def _rms(x):
    return x * lax.rsqrt(jnp.mean(x * x, axis=-1, keepdims=True) + EPS)


SUBSTREAMS = 16
SUB_LEN = SEQ // SUBSTREAMS
IN_ROWS = 512
IN_STEPS = SEQ // IN_ROWS
SUB_ROWS = IN_ROWS // SUBSTREAMS


def _rope_kernel(pos_ref, inv_ref, cos_ref, sin_ref, cos_scr, sin_scr):
    lane = lax.broadcasted_iota(jnp.int32, (1, HEAD_DIM), 1)
    low = lane < HEAD_DIM // 2
    sign = jnp.where(low, -1.0, 1.0).astype(F32)
    rows = 256
    half_seq = SEQ // 2

    def body(c, carry):
        ra = pl.multiple_of(c * rows, rows)
        rb = pl.multiple_of(half_seq + c * rows, rows)
        pos = jnp.where(low, pos_ref[pl.ds(ra, rows), :], pos_ref[pl.ds(rb, rows), :])
        ang = pos.astype(F32) * inv_ref[...]
        cos, sin = jnp.cos(ang), jnp.sin(ang)
        cos_sw = pltpu.roll(cos, HEAD_DIM // 2, axis=1)
        sin_sw = pltpu.roll(sin, HEAD_DIM // 2, axis=1)
        cos_scr[pl.ds(ra, rows), :] = jnp.where(low, cos, cos_sw)
        cos_scr[pl.ds(rb, rows), :] = jnp.where(low, cos_sw, cos)
        sin_scr[pl.ds(ra, rows), :] = jnp.where(low, sin, sin_sw) * sign
        sin_scr[pl.ds(rb, rows), :] = jnp.where(low, sin_sw, sin) * sign
        return carry

    lax.fori_loop(0, half_seq // rows, body, 0)
    for c in range(IN_STEPS):
        for j in range(SUBSTREAMS):
            src = pl.ds(c * IN_ROWS + j, SUB_ROWS, stride=SUBSTREAMS)
            cos_ref[0, j, c] = cos_scr[src, :]
            sin_ref[0, j, c] = sin_scr[src, :]


def _rope_tables(pos, inv2):
    shape = (BATCH, SUBSTREAMS, IN_STEPS, SUB_ROWS, HEAD_DIM)
    out = jax.ShapeDtypeStruct(shape, F32)
    spec = pl.BlockSpec((1,) + shape[1:], lambda b: (b, 0, 0, 0, 0))
    return pl.pallas_call(
        _rope_kernel,
        grid=(BATCH,),
        in_specs=[pl.BlockSpec((SEQ, 1), lambda b: (b, 0)),
                  pl.BlockSpec((1, HEAD_DIM), lambda b: (0, 0))],
        out_specs=[spec, spec],
        out_shape=[out, out],
        scratch_shapes=[pltpu.VMEM((SEQ, HEAD_DIM), F32), pltpu.VMEM((SEQ, HEAD_DIM), F32)],
        compiler_params=pltpu.CompilerParams(dimension_semantics=("parallel",)),
        name="rope_tables",
    )(pos, inv2)


NAT_COL_BLOCKS = (0, 1, 2, 3, 13, 14, 15, 16)
SLAB_XR, SLAB_GR, SLAB_GRNN, SLAB_GATT = 0, 2, 4, 6
Q_COL0, K_COL0, V_COL0 = 4, 7, 10
ACC_SLOTS = 3


def _in_proj_kernel(h_ref, w_ref, cos_ref, sin_ref, qg_ref, kg_ref, zn_ref, zp_ref, acc_scr):
    heads = [slice(hh * HEAD_DIM, (hh + 1) * HEAD_DIM) for hh in range(HEADS_PER_GROUP)]

    def project(col_block, slot):
        acc = jnp.dot(h_ref[...], w_ref[:, col_block * SLAB:(col_block + 1) * SLAB],
                      preferred_element_type=F32)
        for hh, cols in enumerate(heads):
            acc_scr[slot, hh] = acc[:, cols]

    def store_natural(slab, slot):
        activate = SLAB_GR <= slab < SLAB_GRNN
        for hh, cols in enumerate(heads):
            val = acc_scr[slot, hh]
            zn_ref[slab, :, cols] = (jax.nn.gelu(val) if activate else val).astype(BF16)

    def store_substreams(slab, gain_ref, scale, group, slot):
        subs = range(SUBSTREAMS)
        for hh, cols in enumerate(heads):
            vals = [acc_scr[slot, hh, pl.ds(j, SUB_ROWS, stride=SUBSTREAMS), :] for j in subs]
            if gain_ref is not None:
                head = group * HEADS_PER_GROUP + hh
                gain = gain_ref[:, head * HEAD_DIM:(head + 1) * HEAD_DIM]
                ms = [jnp.mean(v * v, axis=-1, keepdims=True) for v in vals]
                inv = [lax.rsqrt(m + EPS) for m in ms]
                ys = [v * i * gain for v, i in zip(vals, inv)]
                rolled = [pltpu.roll(y, HEAD_DIM // 2, axis=1) for y in ys]
                vals = [y * cos_ref[0, j, 0] + r * sin_ref[0, j, 0] for j, y, r in zip(subs, ys, rolled)]
                if scale is not None:
                    vals = [v * scale for v in vals]
            for j, v in zip(subs, vals):
                zp_ref[slab, 0, j, 0, :, cols] = v.astype(BF16)

    natural = [(cb, functools.partial(store_natural, s)) for s, cb in enumerate(NAT_COL_BLOCKS)]
    dilated = []
    for g in range(N_GROUPS):
        dilated.append((Q_COL0 + g, functools.partial(store_substreams, 3 * g, qg_ref, HEAD_DIM ** -0.5, g)))
        dilated.append((K_COL0 + g, functools.partial(store_substreams, 3 * g + 1, kg_ref, None, g)))
    for g in range(N_GROUPS):
        dilated.append((V_COL0 + g, functools.partial(store_substreams, 3 * g + 2, None, None, g)))
    jobs = []
    for idx, job in enumerate(dilated):
        jobs.append(job)
        if idx < len(natural):
            jobs.append(natural[idx])
    ahead = ACC_SLOTS - 1
    for idx in range(min(ahead, len(jobs))):
        project(jobs[idx][0], idx % ACC_SLOTS)
    for idx, (_, finish) in enumerate(jobs):
        if idx + ahead < len(jobs):
            project(jobs[idx + ahead][0], (idx + ahead) % ACC_SLOTS)
        finish(idx % ACC_SLOTS)


def _in_proj(layer, h, w_bf, cos_t, sin_t, q_gain, k_gain):
    gain = pl.BlockSpec((None, 1, ATT_WIDTH), lambda b, c: (layer, 0, 0))
    table = pl.BlockSpec((1, SUBSTREAMS, 1, SUB_ROWS, HEAD_DIM), lambda b, c: (b, 0, c, 0, 0))
    zn, zp = pl.pallas_call(
        _in_proj_kernel,
        grid=(BATCH, IN_STEPS),
        in_specs=[
            pl.BlockSpec((IN_ROWS, D_MODEL), lambda b, c: (b * IN_STEPS + c, 0)),
            pl.BlockSpec((None, D_MODEL, N_IN), lambda b, c: (layer, 0, 0), pipeline_mode=pl.Buffered(1)),
            table, table, gain, gain,
        ],
        out_specs=[
            pl.BlockSpec((len(NAT_COL_BLOCKS), IN_ROWS, SLAB), lambda b, c: (0, b * IN_STEPS + c, 0)),
            pl.BlockSpec((3 * N_GROUPS, 1, SUBSTREAMS, 1, SUB_ROWS, SLAB), lambda b, c: (0, b, 0, c, 0, 0)),
        ],
        out_shape=[
            jax.ShapeDtypeStruct((len(NAT_COL_BLOCKS), M_TOKENS, SLAB), BF16),
            jax.ShapeDtypeStruct((3 * N_GROUPS, BATCH, SUBSTREAMS, IN_STEPS, SUB_ROWS, SLAB), BF16),
        ],
        scratch_shapes=[pltpu.VMEM((ACC_SLOTS, HEADS_PER_GROUP, IN_ROWS, HEAD_DIM), F32)],
        compiler_params=pltpu.CompilerParams(
            dimension_semantics=("parallel", "parallel"), vmem_limit_bytes=VMEM_LIMIT),
        name="in_proj",
    )(h, w_bf, cos_t, sin_t, q_gain, k_gain)
    return zn, zp.reshape(3 * N_GROUPS, M_TOKENS, SLAB)


RNN_COLS = 512
RNN_ROWS = 256
RNN_PAD = SUBLANES


def _rnn_kernel(xr_ref, gr_ref, cw_ref, cb_ref, wa_ref, ba_ref, wx_ref, bx_ref, lam_ref,
                o_ref, xs_scr, a_scr, u_scr):
    xs_scr[0:RNN_PAD, :] = jnp.zeros((RNN_PAD, RNN_COLS), F32)

    def stage(c, carry):
        r0 = pl.multiple_of(c * RNN_ROWS, RNN_ROWS)
        xs_scr[pl.ds(RNN_PAD + r0, RNN_ROWS), :] = xr_ref[0, pl.ds(r0, RNN_ROWS), :].astype(F32)
        return carry

    lax.fori_loop(0, SEQ // RNN_ROWS, stage, 0)

    softplus_neg_lam = jax.nn.softplus(-lam_ref[...])

    def gates(c, carry):
        r0 = pl.multiple_of(c * RNN_ROWS, RNN_ROWS)
        win = xs_scr[pl.ds(r0, RNN_ROWS + RNN_PAD), :]
        xc = cb_ref[...]
        for i in range(RNN_CONV_WIDTH):
            back = RNN_CONV_WIDTH - 1 - i
            tap = win if back == 0 else pltpu.roll(win, back, axis=0)
            xc = xc + tap[RNN_PAD:] * cw_ref[i:i + 1, :]
        xc_bf = xc.astype(BF16)
        for nb in range(RNN_COLS // LRU_BLOCK):
            cols = slice(nb * LRU_BLOCK, (nb + 1) * LRU_BLOCK)
            xb = xc_bf[:, cols]
            r = jax.nn.sigmoid(jnp.dot(xb, wa_ref[nb], preferred_element_type=F32) + ba_ref[:, cols])
            ig = jax.nn.sigmoid(jnp.dot(xb, wx_ref[nb], preferred_element_type=F32) + bx_ref[:, cols])
            log_a = -LRU_C * r * softplus_neg_lam[:, cols]
            a = jnp.exp(log_a)
            mult = jnp.sqrt(-jnp.tanh(log_a) * (a * a + 1.0))
            a_scr[pl.ds(r0, RNN_ROWS), cols] = a
            u_scr[pl.ds(r0, RNN_ROWS), cols] = mult * (ig * xc[:, cols])
        return carry

    lax.fori_loop(0, SEQ // RNN_ROWS, gates, 0)

    row = lax.broadcasted_iota(jnp.int32, (SUBLANES, RNN_COLS), 0)

    def scan(t, carry):
        r0 = pl.multiple_of(t * BF16_ROWS, BF16_ROWS)
        a16 = a_scr[pl.ds(r0, BF16_ROWS), :]
        u16 = u_scr[pl.ds(r0, BF16_ROWS), :]
        hs = []
        for half in range(BF16_ROWS // SUBLANES):
            a = a16[half * SUBLANES:(half + 1) * SUBLANES]
            u = u16[half * SUBLANES:(half + 1) * SUBLANES]
            for s in (1, 2, 4):
                a_prev = jnp.where(row >= s, pltpu.roll(a, s, axis=0), 1.0)
                u_prev = jnp.where(row >= s, pltpu.roll(u, s, axis=0), 0.0)
                u = a * u_prev + u
                a = a * a_prev
            h = a * carry + u
            carry = jnp.broadcast_to(h[SUBLANES - 1:SUBLANES, :], (SUBLANES, RNN_COLS))
            hs.append(h)
        gate = gr_ref[0, pl.ds(r0, BF16_ROWS), :].astype(F32)
        o_ref[pl.ds(r0, BF16_ROWS), :] = (jnp.concatenate(hs, axis=0) * gate).astype(BF16)
        return carry

    lax.fori_loop(0, SEQ // BF16_ROWS, scan, jnp.zeros((SUBLANES, RNN_COLS), F32), unroll=2)


def _rnn(layer, z, conv_w, conv_b, wa_bf, ba, wx_bf, bx, lam):
    n_ct = D_RNN // RNN_COLS
    blocks_per_ct = RNN_COLS // LRU_BLOCK
    vec = pl.BlockSpec((None, 1, RNN_COLS), lambda b, c: (layer, 0, c))
    gate_w = pl.BlockSpec((None, blocks_per_ct, LRU_BLOCK, LRU_BLOCK), lambda b, c: (layer, c, 0, 0))
    return pl.pallas_call(
        _rnn_kernel,
        grid=(BATCH, n_ct),
        in_specs=[
            pl.BlockSpec((1, SEQ, SLAB), lambda b, c: (SLAB_XR + c, b, 0)),
            pl.BlockSpec((1, SEQ, SLAB), lambda b, c: (SLAB_GR + c, b, 0)),
            pl.BlockSpec((None, RNN_CONV_WIDTH, RNN_COLS), lambda b, c: (layer, 0, c)),
            vec, gate_w, vec, gate_w, vec, vec,
        ],
        out_specs=pl.BlockSpec((SEQ, RNN_COLS), lambda b, c: (b, c)),
        out_shape=jax.ShapeDtypeStruct((M_TOKENS, D_RNN), BF16),
        scratch_shapes=[pltpu.VMEM((SEQ + RNN_PAD, RNN_COLS), F32),
                        pltpu.VMEM((SEQ, RNN_COLS), F32),
                        pltpu.VMEM((SEQ, RNN_COLS), F32)],
        compiler_params=pltpu.CompilerParams(
            dimension_semantics=("parallel", "parallel"), vmem_limit_bytes=VMEM_LIMIT),
        name="rnn",
    )(z, z, conv_w, conv_b, wa_bf, ba, wx_bf, bx, lam)


ATT_HEADS = 2
MERGE_ROWS = 256
Q_TILE = {1: 256, 4: 128, 16: 128}


def _tile_geometry(d):
    pieces = SUBSTREAMS // d
    return pieces, Q_TILE[d] // pieces, d


def _attn_masks():
    import numpy as np
    masks = []
    for _, d in DILATED_GROUPS:
        pieces, rows, _ = _tile_geometry(d)
        if pieces == 1:
            pos_q = np.arange(rows)
            first = pos_q[:, None] - pos_q[None, :]
            masks.append((first, None))
            continue
        pos_q = (pieces * np.arange(rows)[None, :] + np.arange(pieces)[:, None]).reshape(-1)
        pos_k2 = (pieces * np.arange(2 * rows)[None, :] + np.arange(pieces)[:, None]).reshape(-1)
        first = pos_q[:, None] - pos_q[None, :]
        band = (pos_q[:, None] + Q_TILE[d]) - pos_k2[None, :]
        masks.append((first, band))
    out = []
    for first, band in masks:
        for dist in (first, band):
            if dist is not None:
                out.append(np.where((dist >= 0) & (dist <= SPAN), 0.0, NEG_INF).astype(np.float32))
    return out


def _attn_kernel(q0, k0, v0, q1, k1, v1, q2, k2, v2, m0f, m0b, m1f, m1b, m2f, o_ref, o_scr, l_scr, y_scr):
    refs = ((q0, k0, v0, m0f, m0b), (q1, k1, v1, m1f, m1b), (q2, k2, v2, m2f, None))

    def tile_rows(d, blk, stream, prev):
        pieces, rows, step = _tile_geometry(d)
        start = pl.multiple_of((blk - prev) * rows, rows)
        return [pl.ds((stream + u * step) * SUB_LEN + start, rows * (1 + prev)) for u in range(pieces)]

    def gather(ref, slices, cols):
        parts = [ref[0, sl, cols] for sl in slices]
        return parts[0] if len(parts) == 1 else jnp.concatenate(parts, axis=0)

    def attend(g, tiles):
        q_ref, k_ref, v_ref, first_ref, band_ref = refs[g]
        d = DILATED_GROUPS[g][1]
        pieces, rows, _ = _tile_geometry(d)
        chains = [(t, hh) for t in tiles for hh in range(ATT_HEADS)]
        col = lambda hh: slice(hh * HEAD_DIM, (hh + 1) * HEAD_DIM)
        s = []
        for (blk, stream, prev), hh in chains:
            q = gather(q_ref, tile_rows(d, blk, stream, 0), col(hh))
            k = gather(k_ref, tile_rows(d, blk, stream, int(prev)), col(hh))
            bias = band_ref[...] if prev else first_ref[...]
            s.append(lax.dot_general(q, k, (((1,), (1,)), ((), ())), preferred_element_type=F32) + bias)
        m = [jnp.max(x, axis=-1, keepdims=True) for x in s]
        p = [jnp.exp(x - mx) for x, mx in zip(s, m)]
        den = [jnp.sum(x, axis=-1, keepdims=True) for x in p]
        out = []
        for ((blk, stream, prev), hh), px in zip(chains, p):
            v = gather(v_ref, tile_rows(d, blk, stream, int(prev)), col(hh))
            out.append(jnp.dot(px.astype(BF16), v, preferred_element_type=F32))
        for ((blk, stream, prev), hh), ox, mx, dx in zip(chains, out, m, den):
            ox = ox / dx
            lse = jnp.broadcast_to(mx + jnp.log(dx), ox.shape)
            for u, sl in enumerate(tile_rows(d, blk, stream, 0)):
                o_scr[g * ATT_HEADS + hh, sl, :] = ox[u * rows:(u + 1) * rows]
                l_scr[g * ATT_HEADS + hh, sl, :] = lse[u * rows:(u + 1) * rows]

    attend(0, [(0, 0, False)])

    def body0(blk, carry):
        attend(0, [(blk, 0, True)])
        return carry

    lax.fori_loop(1, SEQ // Q_TILE[1], body0, 0)

    def body1(stream, carry):
        attend(1, [(0, stream, False)] + [(blk, stream, True) for blk in range(1, SEQ // 4 // Q_TILE[4])])
        return carry

    lax.fori_loop(0, 4, body1, 0)

    def body2(t, carry):
        attend(2, [(0, t * 4 + u, False) for u in range(4)])
        return carry

    lax.fori_loop(0, SUBSTREAMS // 4, body2, 0)

    def merge(j, carry):
        rows = pl.ds(pl.multiple_of(j * SUB_LEN, SUB_LEN), SUB_LEN)
        for hh in range(ATT_HEADS):
            l = [l_scr[g * ATT_HEADS + hh, rows, :] for g in range(N_GROUPS)]
            o = [o_scr[g * ATT_HEADS + hh, rows, :] for g in range(N_GROUPS)]
            top = jnp.maximum(jnp.maximum(l[0], l[1]), l[2])
            e = [jnp.exp(lg - top) for lg in l]
            y_scr[hh, pl.ds(j, SUB_LEN, stride=SUBSTREAMS), :] = (
                (e[0] * o[0] + e[1] * o[1] + e[2] * o[2]) / (e[0] + e[1] + e[2]))
        return carry

    lax.fori_loop(0, SUBSTREAMS, merge, 0)

    def emit(c, carry):
        rows = pl.ds(pl.multiple_of(c * MERGE_ROWS, MERGE_ROWS), MERGE_ROWS)
        for hh in range(ATT_HEADS):
            o_ref[rows, hh * HEAD_DIM:(hh + 1) * HEAD_DIM] = y_scr[hh, rows, :].astype(BF16)
        return carry

    lax.fori_loop(0, SEQ // MERGE_ROWS, emit, 0)


def _attn(zp):
    width = ATT_HEADS * HEAD_DIM
    specs = []
    for g in range(N_GROUPS):
        for t in range(3):
            specs.append(pl.BlockSpec((1, SEQ, width), lambda b, h, s=3 * g + t: (s, b, h)))
    masks = [jnp.asarray(mk) for mk in _attn_masks()]
    specs += [pl.BlockSpec(mk.shape, lambda b, h: (0, 0)) for mk in masks]
    planes = N_GROUPS * ATT_HEADS
    return pl.pallas_call(
        _attn_kernel,
        grid=(BATCH, HEADS_PER_GROUP // ATT_HEADS),
        in_specs=specs,
        out_specs=pl.BlockSpec((SEQ, width), lambda b, h: (b, h)),
        out_shape=jax.ShapeDtypeStruct((M_TOKENS, ATT_OUT_WIDTH), BF16),
        scratch_shapes=[pltpu.VMEM((planes, SEQ, HEAD_DIM), F32),
                        pltpu.VMEM((planes, SEQ, HEAD_DIM), F32),
                        pltpu.VMEM((ATT_HEADS, SEQ, HEAD_DIM), F32)],
        compiler_params=pltpu.CompilerParams(
            dimension_semantics=("parallel", "parallel"), vmem_limit_bytes=VMEM_LIMIT),
        name="attn",
    )(*([zp] * 9), *masks)


MO_ROWS = 512


def _merge_out_kernel(yr_ref, ya_ref, grnn_ref, gatt_ref, x_ref, pr_ref, pa_ref, wo_ref, ln2_ref,
                      xo_ref, h2_ref):
    t_rnn = jnp.dot(yr_ref[...], pr_ref[...], preferred_element_type=F32)
    t_att = jnp.dot(ya_ref[...], pa_ref[...], preferred_element_type=F32)
    halves = []
    for hf in range(D_MODEL // SLAB):
        cols = slice(hf * SLAB, (hf + 1) * SLAB)
        g_rnn = jax.nn.sigmoid(grnn_ref[hf].astype(F32))
        g_att = jax.nn.sigmoid(gatt_ref[hf].astype(F32))
        halves.append((g_rnn * t_rnn[:, cols] + g_att * t_att[:, cols]).astype(BF16))
    merged = jnp.concatenate(halves, axis=1)
    xn = x_ref[...] + jnp.dot(merged, wo_ref[...], preferred_element_type=F32)
    xo_ref[...] = xn
    h2_ref[...] = (_rms(xn) * ln2_ref[...]).astype(BF16)


def _merge_out(layer, y_rnn, y_att, z, x2, pr_bf, pa_bf, wo_bf, ln2):
    slabs_per_gate = D_MODEL // SLAB
    full = lambda shape: pl.BlockSpec((None,) + shape, lambda i: (layer, 0, 0))
    return pl.pallas_call(
        _merge_out_kernel,
        grid=(M_TOKENS // MO_ROWS,),
        in_specs=[
            pl.BlockSpec((MO_ROWS, D_RNN), lambda i: (i, 0)),
            pl.BlockSpec((MO_ROWS, ATT_OUT_WIDTH), lambda i: (i, 0)),
            pl.BlockSpec((slabs_per_gate, MO_ROWS, SLAB), lambda i: (SLAB_GRNN // slabs_per_gate, i, 0)),
            pl.BlockSpec((slabs_per_gate, MO_ROWS, SLAB), lambda i: (SLAB_GATT // slabs_per_gate, i, 0)),
            pl.BlockSpec((MO_ROWS, D_MODEL), lambda i: (i, 0)),
            full((D_RNN, D_MODEL)), full((ATT_OUT_WIDTH, D_MODEL)), full((D_MODEL, D_MODEL)),
            full((1, D_MODEL)),
        ],
        out_specs=[pl.BlockSpec((MO_ROWS, D_MODEL), lambda i: (i, 0)),
                   pl.BlockSpec((MO_ROWS, D_MODEL), lambda i: (i, 0))],
        out_shape=[jax.ShapeDtypeStruct((M_TOKENS, D_MODEL), F32),
                   jax.ShapeDtypeStruct((M_TOKENS, D_MODEL), BF16)],
        compiler_params=pltpu.CompilerParams(
            dimension_semantics=("parallel",), vmem_limit_bytes=VMEM_LIMIT),
        name="merge_out",
    )(y_rnn, y_att, z, z, x2, pr_bf, pa_bf, wo_bf, ln2)


FFN_ROWS = 1024
FFN_COLS = 512
FFN_CHUNK = 512
FFN_HALO = BF16_ROWS


def _ffn_kernel(emit_h, h2_ref, halo_ref, wup_ref, cw_ref, cb_ref, wd_ref, x_ref, *rest):
    if emit_h:
        ln_ref, o_ref, h_ref, hbuf = rest
    else:
        o_ref, hbuf = rest
    seq_start = pl.program_id(0) % (SEQ // FFN_ROWS) == 0

    @pl.when(seq_start)
    def _():
        hbuf[0:FFN_HALO, :] = jnp.zeros((FFN_HALO, D_MODEL), BF16)

    @pl.when(jnp.logical_not(seq_start))
    def _():
        hbuf[0:FFN_HALO, :] = halo_ref[...]

    hbuf[FFN_HALO:, :] = h2_ref[...]
    n_k = D_FF // FFN_COLS

    def up(hb, k):
        cols = [slice(base + k * FFN_COLS, base + (k + 1) * FFN_COLS) for base in (0, D_FF)]
        return [(jnp.dot(hb, wup_ref[:, c], preferred_element_type=F32), c) for c in cols]

    def conv(u, cols):
        out = cb_ref[:, cols]
        for t in range(FFN_CONV_WIDTH):
            back = FFN_CONV_WIDTH - 1 - t
            tap = u if back == 0 else pltpu.roll(u, back, axis=0)
            out = out + tap[FFN_HALO:] * cw_ref[t:t + 1, cols]
        return out

    def chunk(c, carry):
        r0 = pl.multiple_of(c * FFN_CHUNK, FFN_CHUNK)
        hb = hbuf[pl.ds(r0, FFN_CHUNK + FFN_HALO), :]
        acc = x_ref[pl.ds(r0, FFN_CHUNK), :]
        pre = up(hb, 0)
        for k in range(n_k):
            nxt = up(hb, k + 1) if k + 1 < n_k else None
            (ug, gcols), (uv, vcols) = pre
            act = (jax.nn.gelu(conv(ug, gcols)) * conv(uv, vcols)).astype(BF16)
            acc = acc + jnp.dot(act, wd_ref[k * FFN_COLS:(k + 1) * FFN_COLS, :], preferred_element_type=F32)
            pre = nxt
        o_ref[pl.ds(r0, FFN_CHUNK), :] = acc
        if emit_h:
            h_ref[pl.ds(r0, FFN_CHUNK), :] = (_rms(acc) * ln_ref[...]).astype(BF16)
        return carry

    lax.fori_loop(0, FFN_ROWS // FFN_CHUNK, chunk, 0)


def _ffn(layer, h2, x2, wup_bf, conv_w, conv_b, wd_bf, ln1):
    emit_h = layer + 1 < DEPTH
    halo_blocks = FFN_ROWS // FFN_HALO
    resident = lambda shape: pl.BlockSpec((None,) + shape, lambda i: (layer, 0, 0),
                                          pipeline_mode=pl.Buffered(1))
    tile = pl.BlockSpec((FFN_ROWS, D_MODEL), lambda i: (i, 0))
    in_specs = [
        tile,
        pl.BlockSpec((FFN_HALO, D_MODEL), lambda i: (jnp.maximum(i * halo_blocks - 1, 0), 0)),
        resident((D_MODEL, 2 * D_FF)),
        resident((FFN_CONV_WIDTH, 2 * D_FF)),
        resident((1, 2 * D_FF)),
        resident((D_FF, D_MODEL)),
        tile,
    ]
    args = [h2, h2, wup_bf, conv_w, conv_b, wd_bf, x2]
    out_specs = [tile]
    out_shape = [jax.ShapeDtypeStruct((M_TOKENS, D_MODEL), F32)]
    if emit_h:
        in_specs.append(pl.BlockSpec((None, 1, D_MODEL), lambda i: (layer + 1, 0, 0)))
        args.append(ln1)
        out_specs.append(tile)
        out_shape.append(jax.ShapeDtypeStruct((M_TOKENS, D_MODEL), BF16))
    outs = pl.pallas_call(
        functools.partial(_ffn_kernel, emit_h),
        grid=(M_TOKENS // FFN_ROWS,),
        in_specs=in_specs,
        out_specs=out_specs,
        out_shape=out_shape,
        scratch_shapes=[pltpu.VMEM((FFN_ROWS + FFN_HALO, D_MODEL), BF16)],
        compiler_params=pltpu.CompilerParams(
            dimension_semantics=("parallel",), vmem_limit_bytes=VMEM_LIMIT),
        name="ffn",
    )(*args)
    return (outs[0], outs[1]) if emit_h else (outs[0], None)


def _norm_kernel(x_ref, ln_ref, h_ref):
    h_ref[...] = (_rms(x_ref[...]) * ln_ref[...]).astype(BF16)


def _first_norm(x2, ln1):
    tile = pl.BlockSpec((MO_ROWS, D_MODEL), lambda i: (i, 0))
    return pl.pallas_call(
        _norm_kernel,
        grid=(M_TOKENS // MO_ROWS,),
        in_specs=[tile, pl.BlockSpec((None, 1, D_MODEL), lambda i: (0, 0, 0))],
        out_specs=tile,
        out_shape=jax.ShapeDtypeStruct((M_TOKENS, D_MODEL), BF16),
        compiler_params=pltpu.CompilerParams(dimension_semantics=("parallel",)),
        name="first_norm",
    )(x2, ln1)


def kernel(x, positions, ln1_g, w_in, rnn_conv_w, rnn_conv_b, lru_wa, lru_ba, lru_wx, lru_bx, lru_lambda, q_norm_g, k_norm_g, proj_rnn, proj_attn, w_out, ln2_g, w_up, ffn_conv_w, ffn_conv_b, w_down):
    x2 = x.reshape(M_TOKENS, D_MODEL)
    half = HEAD_DIM // 2
    inv_freq = ROPE_THETA ** (-jnp.arange(half, dtype=F32) / half)
    inv2 = jnp.concatenate([inv_freq, inv_freq]).reshape(1, HEAD_DIM)
    cos_t, sin_t = _rope_tables(positions.reshape(M_TOKENS, 1), inv2)

    rows = lambda v: v.reshape(DEPTH, 1, -1)
    w_in, lru_wa, lru_wx, proj_rnn, proj_attn, w_out, w_up, w_down = (
        w.astype(BF16) for w in (w_in, lru_wa, lru_wx, proj_rnn, proj_attn, w_out, w_up, w_down))
    ln1, ln2, q_gain, k_gain = rows(ln1_g), rows(ln2_g), rows(q_norm_g), rows(k_norm_g)
    conv_b, ba, bx, lam, ffn_b = (rows(v) for v in (rnn_conv_b, lru_ba, lru_bx, lru_lambda, ffn_conv_b))

    h = _first_norm(x2, ln1)
    for l in range(DEPTH):
        zn, zp = _in_proj(l, h, w_in, cos_t, sin_t, q_gain, k_gain)
        y_rnn = _rnn(l, zn, rnn_conv_w, conv_b, lru_wa, ba, lru_wx, bx, lam)
        y_att = _attn(zp)
        x2, h2 = _merge_out(l, y_rnn, y_att, zn, x2, proj_rnn, proj_attn, w_out, ln2)
        x2, h = _ffn(l, h2, x2, w_up, ffn_conv_w, ffn_b, w_down, ln1)
    return x2.reshape(BATCH, SEQ, D_MODEL)
```

```python
import functools

import jax
import jax.numpy as jnp
from jax import lax
from jax.experimental import pallas as pl
from jax.experimental.pallas import tpu as pltpu

D_MODEL = 1024
BATCH = 8
SEQ = 2048
DEPTH = 4
M_TOKENS = BATCH * SEQ

D_RNN = D_MODEL
N_LRU_BLOCKS = 8
LRU_BLOCK = D_RNN // N_LRU_BLOCKS
RNN_CONV_WIDTH = 4
LRU_C = 8.0
HEAD_DIM = 128
HEADS_PER_GROUP = 4
DILATED_GROUPS = ((128, 1), (512, 4), (2048, 16))
N_GROUPS = len(DILATED_GROUPS)
N_ATT_HEADS = N_GROUPS * HEADS_PER_GROUP
ATT_WIDTH = N_ATT_HEADS * HEAD_DIM
ATT_OUT_WIDTH = HEADS_PER_GROUP * HEAD_DIM
SPAN = 128
ROPE_THETA = 10000.0
D_FF = 3 * D_MODEL
FFN_CONV_WIDTH = 3
EPS = 1e-6
NEG_INF = -1e30
N_IN = 2 * D_RNN + 3 * ATT_WIDTH + 2 * D_MODEL

F32 = jnp.float32
BF16 = jnp.bfloat16

SUBLANES = 8
BF16_ROWS = 16
SLAB = 512
VMEM_LIMIT = 56 * 1024 * 1024


def _rms(x):
    return x * lax.rsqrt(jnp.mean(x * x, axis=-1, keepdims=True) + EPS)


SUBSTREAMS = 16
SUB_LEN = SEQ // SUBSTREAMS
IN_ROWS = 512
IN_STEPS = SEQ // IN_ROWS
SUB_ROWS = IN_ROWS // SUBSTREAMS


def _rope_kernel(pos_ref, inv_ref, cos_ref, sin_ref, cos_scr, sin_scr):
    lane = lax.broadcasted_iota(jnp.int32, (1, HEAD_DIM), 1)
    low = lane < HEAD_DIM // 2
    sign = jnp.where(low, -1.0, 1.0).astype(F32)
    rows = 256
    half_seq = SEQ // 2

    def body(c, carry):
        ra = pl.multiple_of(c * rows, rows)
        rb = pl.multiple_of(half_seq + c * rows, rows)
        pos = jnp.where(low, pos_ref[pl.ds(ra, rows), :], pos_ref[pl.ds(rb, rows), :])
        ang = pos.astype(F32) * inv_ref[...]
        cos, sin = jnp.cos(ang), jnp.sin(ang)
        cos_sw = pltpu.roll(cos, HEAD_DIM // 2, axis=1)
        sin_sw = pltpu.roll(sin, HEAD_DIM // 2, axis=1)
        cos_scr[pl.ds(ra, rows), :] = jnp.where(low, cos, cos_sw)
        cos_scr[pl.ds(rb, rows), :] = jnp.where(low, cos_sw, cos)
        sin_scr[pl.ds(ra, rows), :] = jnp.where(low, sin, sin_sw) * sign
        sin_scr[pl.ds(rb, rows), :] = jnp.where(low, sin_sw, sin) * sign
        return carry

    lax.fori_loop(0, half_seq // rows, body, 0)
    for c in range(IN_STEPS):
        for j in range(SUBSTREAMS):
            src = pl.ds(c * IN_ROWS + j, SUB_ROWS, stride=SUBSTREAMS)
            cos_ref[0, j, c] = cos_scr[src, :]
            sin_ref[0, j, c] = sin_scr[src, :]


def _rope_tables(pos, inv2):
    shape = (BATCH, SUBSTREAMS, IN_STEPS, SUB_ROWS, HEAD_DIM)
    out = jax.ShapeDtypeStruct(shape, F32)
    spec = pl.BlockSpec((1,) + shape[1:], lambda b: (b, 0, 0, 0, 0))
    return pl.pallas_call(
        _rope_kernel,
        grid=(BATCH,),
        in_specs=[pl.BlockSpec((SEQ, 1), lambda b: (b, 0)),
                  pl.BlockSpec((1, HEAD_DIM), lambda b: (0, 0))],
        out_specs=[spec, spec],
        out_shape=[out, out],
        scratch_shapes=[pltpu.VMEM((SEQ, HEAD_DIM), F32), pltpu.VMEM((SEQ, HEAD_DIM), F32)],
        compiler_params=pltpu.CompilerParams(dimension_semantics=("parallel",)),
        name="rope_tables",
    )(pos, inv2)


NAT_COL_BLOCKS = (0, 1, 2, 3, 13, 14, 15, 16)
SLAB_XR, SLAB_GR, SLAB_GRNN, SLAB_GATT = 0, 2, 4, 6
Q_COL0, K_COL0, V_COL0 = 4, 7, 10
ACC_SLOTS = 3


def _in_proj_kernel(h_ref, w_ref, cos_ref, sin_ref, qg_ref, kg_ref, zn_ref, zp_ref, acc_scr):
    heads = [slice(hh * HEAD_DIM, (hh + 1) * HEAD_DIM) for hh in range(HEADS_PER_GROUP)]

    def project(col_block, slot):
        acc = jnp.dot(h_ref[...], w_ref[:, col_block * SLAB:(col_block + 1) * SLAB],
                      preferred_element_type=F32)
        for hh, cols in enumerate(heads):
            acc_scr[slot, hh] = acc[:, cols]

    def store_natural(slab, slot):
        activate = SLAB_GR <= slab < SLAB_GRNN
        for hh, cols in enumerate(heads):
            val = acc_scr[slot, hh]
            zn_ref[slab, :, cols] = (jax.nn.gelu(val) if activate else val).astype(BF16)

    def store_substreams(slab, gain_ref, scale, group, slot):
        subs = range(SUBSTREAMS)
        for hh, cols in enumerate(heads):
            vals = [acc_scr[slot, hh, pl.ds(j, SUB_ROWS, stride=SUBSTREAMS), :] for j in subs]
            if gain_ref is not None:
                head = group * HEADS_PER_GROUP + hh
                gain = gain_ref[:, head * HEAD_DIM:(head + 1) * HEAD_DIM]
                ms = [jnp.mean(v * v, axis=-1, keepdims=True) for v in vals]
                inv = [lax.rsqrt(m + EPS) for m in ms]
                ys = [v * i * gain for v, i in zip(vals, inv)]
                rolled = [pltpu.roll(y, HEAD_DIM // 2, axis=1) for y in ys]
                vals = [y * cos_ref[0, j, 0] + r * sin_ref[0, j, 0] for j, y, r in zip(subs, ys, rolled)]
                if scale is not None:
                    vals = [v * scale for v in vals]
            for j, v in zip(subs, vals):
                zp_ref[slab, 0, j, 0, :, cols] = v.astype(BF16)

    natural = [(cb, functools.partial(store_natural, s)) for s, cb in enumerate(NAT_COL_BLOCKS)]
    dilated = []
    for g in range(N_GROUPS):
        dilated.append((Q_COL0 + g, functools.partial(store_substreams, 3 * g, qg_ref, HEAD_DIM ** -0.5, g)))
        dilated.append((K_COL0 + g, functools.partial(store_substreams, 3 * g + 1, kg_ref, None, g)))
    for g in range(N_GROUPS):
        dilated.append((V_COL0 + g, functools.partial(store_substreams, 3 * g + 2, None, None, g)))
    jobs = []
    for idx, job in enumerate(dilated):
        jobs.append(job)
        if idx < len(natural):
            jobs.append(natural[idx])
    ahead = ACC_SLOTS - 1
    for idx in range(min(ahead, len(jobs))):
        project(jobs[idx][0], idx % ACC_SLOTS)
    for idx, (_, finish) in enumerate(jobs):
        if idx + ahead < len(jobs):
            project(jobs[idx + ahead][0], (idx + ahead) % ACC_SLOTS)
        finish(idx % ACC_SLOTS)


def _in_proj(layer, h, w_bf, cos_t, sin_t, q_gain, k_gain):
    gain = pl.BlockSpec((None, 1, ATT_WIDTH), lambda b, c: (layer, 0, 0))
    table = pl.BlockSpec((1, SUBSTREAMS, 1, SUB_ROWS, HEAD_DIM), lambda b, c: (b, 0, c, 0, 0))
    zn, zp = pl.pallas_call(
        _in_proj_kernel,
        grid=(BATCH, IN_STEPS),
        in_specs=[
            pl.BlockSpec((IN_ROWS, D_MODEL), lambda b, c: (b * IN_STEPS + c, 0)),
            pl.BlockSpec((None, D_MODEL, N_IN), lambda b, c: (layer, 0, 0), pipeline_mode=pl.Buffered(1)),
            table, table, gain, gain,
        ],
        out_specs=[
            pl.BlockSpec((len(NAT_COL_BLOCKS), IN_ROWS, SLAB), lambda b, c: (0, b * IN_STEPS + c, 0)),
            pl.BlockSpec((3 * N_GROUPS, 1, SUBSTREAMS, 1, SUB_ROWS, SLAB), lambda b, c: (0, b, 0, c, 0, 0)),
        ],
        out_shape=[
            jax.ShapeDtypeStruct((len(NAT_COL_BLOCKS), M_TOKENS, SLAB), BF16),
            jax.ShapeDtypeStruct((3 * N_GROUPS, BATCH, SUBSTREAMS, IN_STEPS, SUB_ROWS, SLAB), BF16),
        ],
        scratch_shapes=[pltpu.VMEM((ACC_SLOTS, HEADS_PER_GROUP, IN_ROWS, HEAD_DIM), F32)],
        compiler_params=pltpu.CompilerParams(
            dimension_semantics=("parallel", "parallel"), vmem_limit_bytes=VMEM_LIMIT),
        name="in_proj",
    )(h, w_bf, cos_t, sin_t, q_gain, k_gain)
    return zn, zp.reshape(3 * N_GROUPS, M_TOKENS, SLAB)


RNN_COLS = 512
RNN_ROWS = 256
RNN_PAD = SUBLANES


def _rnn_kernel(xr_ref, gr_ref, cw_ref, cb_ref, wa_ref, ba_ref, wx_ref, bx_ref, lam_ref,
                o_ref, xs_scr, a_scr, u_scr):
    xs_scr[0:RNN_PAD, :] = jnp.zeros((RNN_PAD, RNN_COLS), F32)

    def stage(c, carry):
        r0 = pl.multiple_of(c * RNN_ROWS, RNN_ROWS)
        xs_scr[pl.ds(RNN_PAD + r0, RNN_ROWS), :] = xr_ref[0, pl.ds(r0, RNN_ROWS), :].astype(F32)
        return carry

    lax.fori_loop(0, SEQ // RNN_ROWS, stage, 0)

    softplus_neg_lam = jax.nn.softplus(-lam_ref[...])

    def gates(c, carry):
        r0 = pl.multiple_of(c * RNN_ROWS, RNN_ROWS)
        win = xs_scr[pl.ds(r0, RNN_ROWS + RNN_PAD), :]
        xc = cb_ref[...]
        for i in range(RNN_CONV_WIDTH):
            back = RNN_CONV_WIDTH - 1 - i
            tap = win if back == 0 else pltpu.roll(win, back, axis=0)
            xc = xc + tap[RNN_PAD:] * cw_ref[i:i + 1, :]
        xc_bf = xc.astype(BF16)
        for nb in range(RNN_COLS // LRU_BLOCK):
            cols = slice(nb * LRU_BLOCK, (nb + 1) * LRU_BLOCK)
            xb = xc_bf[:, cols]
            r = jax.nn.sigmoid(jnp.dot(xb, wa_ref[nb], preferred_element_type=F32) + ba_ref[:, cols])
            ig = jax.nn.sigmoid(jnp.dot(xb, wx_ref[nb], preferred_element_type=F32) + bx_ref[:, cols])
            log_a = -LRU_C * r * softplus_neg_lam[:, cols]
            a = jnp.exp(log_a)
            mult = jnp.sqrt(-jnp.tanh(log_a) * (a * a + 1.0))
            a_scr[pl.ds(r0, RNN_ROWS), cols] = a
            u_scr[pl.ds(r0, RNN_ROWS), cols] = mult * (ig * xc[:, cols])
        return carry

    lax.fori_loop(0, SEQ // RNN_ROWS, gates, 0)

    row = lax.broadcasted_iota(jnp.int32, (SUBLANES, RNN_COLS), 0)

    def scan(t, carry):
        r0 = pl.multiple_of(t * BF16_ROWS, BF16_ROWS)
        a16 = a_scr[pl.ds(r0, BF16_ROWS), :]
        u16 = u_scr[pl.ds(r0, BF16_ROWS), :]
        hs = []
        for half in range(BF16_ROWS // SUBLANES):
            a = a16[half * SUBLANES:(half + 1) * SUBLANES]
            u = u16[half * SUBLANES:(half + 1) * SUBLANES]
            for s in (1, 2, 4):
                a_prev = jnp.where(row >= s, pltpu.roll(a, s, axis=0), 1.0)
                u_prev = jnp.where(row >= s, pltpu.roll(u, s, axis=0), 0.0)
                u = a * u_prev + u
                a = a * a_prev
            h = a * carry + u
            carry = jnp.broadcast_to(h[SUBLANES - 1:SUBLANES, :], (SUBLANES, RNN_COLS))
            hs.append(h)
        gate = gr_ref[0, pl.ds(r0, BF16_ROWS), :].astype(F32)
        o_ref[pl.ds(r0, BF16_ROWS), :] = (jnp.concatenate(hs, axis=0) * gate).astype(BF16)
        return carry

    lax.fori_loop(0, SEQ // BF16_ROWS, scan, jnp.zeros((SUBLANES, RNN_COLS), F32), unroll=2)


def _rnn(layer, z, conv_w, conv_b, wa_bf, ba, wx_bf, bx, lam):
    n_ct = D_RNN // RNN_COLS
    blocks_per_ct = RNN_COLS // LRU_BLOCK
    vec = pl.BlockSpec((None, 1, RNN_COLS), lambda b, c: (layer, 0, c))
    gate_w = pl.BlockSpec((None, blocks_per_ct, LRU_BLOCK, LRU_BLOCK), lambda b, c: (layer, c, 0, 0))
    return pl.pallas_call(
        _rnn_kernel,
        grid=(BATCH, n_ct),
        in_specs=[
            pl.BlockSpec((1, SEQ, SLAB), lambda b, c: (SLAB_XR + c, b, 0)),
            pl.BlockSpec((1, SEQ, SLAB), lambda b, c: (SLAB_GR + c, b, 0)),
            pl.BlockSpec((None, RNN_CONV_WIDTH, RNN_COLS), lambda b, c: (layer, 0, c)),
            vec, gate_w, vec, gate_w, vec, vec,
        ],
        out_specs=pl.BlockSpec((SEQ, RNN_COLS), lambda b, c: (b, c)),
        out_shape=jax.ShapeDtypeStruct((M_TOKENS, D_RNN), BF16),
        scratch_shapes=[pltpu.VMEM((SEQ + RNN_PAD, RNN_COLS), F32),
                        pltpu.VMEM((SEQ, RNN_COLS), F32),
                        pltpu.VMEM((SEQ, RNN_COLS), F32)],
        compiler_params=pltpu.CompilerParams(
            dimension_semantics=("parallel", "parallel"), vmem_limit_bytes=VMEM_LIMIT),
        name="rnn",
    )(z, z, conv_w, conv_b, wa_bf, ba, wx_bf, bx, lam)


ATT_HEADS = 2
MERGE_ROWS = 256
Q_TILE = {1: 256, 4: 128, 16: 128}


def _tile_geometry(d):
    pieces = SUBSTREAMS // d
    return pieces, Q_TILE[d] // pieces, d


def _attn_masks():
    import numpy as np
    masks = []
    for _, d in DILATED_GROUPS:
        pieces, rows, _ = _tile_geometry(d)
        if pieces == 1:
            pos_q = np.arange(rows)
            first = pos_q[:, None] - pos_q[None, :]
            masks.append((first, None))
            continue
        pos_q = (pieces * np.arange(rows)[None, :] + np.arange(pieces)[:, None]).reshape(-1)
        pos_k2 = (pieces * np.arange(2 * rows)[None, :] + np.arange(pieces)[:, None]).reshape(-1)
        first = pos_q[:, None] - pos_q[None, :]
        band = (pos_q[:, None] + Q_TILE[d]) - pos_k2[None, :]
        masks.append((first, band))
    out = []
    for first, band in masks:
        for dist in (first, band):
            if dist is not None:
                out.append(np.where((dist >= 0) & (dist <= SPAN), 0.0, NEG_INF).astype(np.float32))
    return out


def _attn_kernel(q0, k0, v0, q1, k1, v1, q2, k2, v2, m0f, m0b, m1f, m1b, m2f, o_ref, o_scr, l_scr, y_scr):
    refs = ((q0, k0, v0, m0f, m0b), (q1, k1, v1, m1f, m1b), (q2, k2, v2, m2f, None))

    def tile_rows(d, blk, stream, prev):
        pieces, rows, step = _tile_geometry(d)
        start = pl.multiple_of((blk - prev) * rows, rows)
        return [pl.ds((stream + u * step) * SUB_LEN + start, rows * (1 + prev)) for u in range(pieces)]

    def gather(ref, slices, cols):
        parts = [ref[0, sl, cols] for sl in slices]
        return parts[0] if len(parts) == 1 else jnp.concatenate(parts, axis=0)

    def attend(g, tiles):
        q_ref, k_ref, v_ref, first_ref, band_ref = refs[g]
        d = DILATED_GROUPS[g][1]
        pieces, rows, _ = _tile_geometry(d)
        chains = [(t, hh) for t in tiles for hh in range(ATT_HEADS)]
        col = lambda hh: slice(hh * HEAD_DIM, (hh + 1) * HEAD_DIM)
        s = []
        for (blk, stream, prev), hh in chains:
            q = gather(q_ref, tile_rows(d, blk, stream, 0), col(hh))
            k = gather(k_ref, tile_rows(d, blk, stream, int(prev)), col(hh))
            bias = band_ref[...] if prev else first_ref[...]
            s.append(lax.dot_general(q, k, (((1,), (1,)), ((), ())), preferred_element_type=F32) + bias)
        m = [jnp.max(x, axis=-1, keepdims=True) for x in s]
        p = [jnp.exp(x - mx) for x, mx in zip(s, m)]
        den = [jnp.sum(x, axis=-1, keepdims=True) for x in p]
        out = []
        for ((blk, stream, prev), hh), px in zip(chains, p):
            v = gather(v_ref, tile_rows(d, blk, stream, int(prev)), col(hh))
            out.append(jnp.dot(px.astype(BF16), v, preferred_element_type=F32))
        for ((blk, stream, prev), hh), ox, mx, dx in zip(chains, out, m, den):
            ox = ox / dx
            lse = jnp.broadcast_to(mx + jnp.log(dx), ox.shape)
            for u, sl in enumerate(tile_rows(d, blk, stream, 0)):
                o_scr[g * ATT_HEADS + hh, sl, :] = ox[u * rows:(u + 1) * rows]
                l_scr[g * ATT_HEADS + hh, sl, :] = lse[u * rows:(u + 1) * rows]

    attend(0, [(0, 0, False)])

    attend(0, [(1, 0, True)])

    def body0(t, carry):
        attend(0, [(2 * t, 0, True), (2 * t + 1, 0, True)])
        return carry

    lax.fori_loop(1, SEQ // Q_TILE[1] // 2, body0, 0)

    def body1(t, carry):
        tiles = []
        for stream in (2 * t, 2 * t + 1):
            tiles += [(0, stream, False)] + [(blk, stream, True) for blk in range(1, SEQ // 4 // Q_TILE[4])]
        attend(1, tiles)
        return carry

    lax.fori_loop(0, 2, body1, 0)

    def body2(t, carry):
        attend(2, [(0, t * 8 + u, False) for u in range(8)])
        return carry

    lax.fori_loop(0, SUBSTREAMS // 8, body2, 0)

    def merge(j, carry):
        rows = pl.ds(pl.multiple_of(j * SUB_LEN, SUB_LEN), SUB_LEN)
        for hh in range(ATT_HEADS):
            l = [l_scr[g * ATT_HEADS + hh, rows, :] for g in range(N_GROUPS)]
            o = [o_scr[g * ATT_HEADS + hh, rows, :] for g in range(N_GROUPS)]
            top = jnp.maximum(jnp.maximum(l[0], l[1]), l[2])
            e = [jnp.exp(lg - top) for lg in l]
            y_scr[hh, pl.ds(j, SUB_LEN, stride=SUBSTREAMS), :] = (
                (e[0] * o[0] + e[1] * o[1] + e[2] * o[2]) / (e[0] + e[1] + e[2]))
        return carry

    lax.fori_loop(0, SUBSTREAMS, merge, 0)

    def emit(c, carry):
        rows = pl.ds(pl.multiple_of(c * MERGE_ROWS, MERGE_ROWS), MERGE_ROWS)
        for hh in range(ATT_HEADS):
            o_ref[rows, hh * HEAD_DIM:(hh + 1) * HEAD_DIM] = y_scr[hh, rows, :].astype(BF16)
        return carry

    lax.fori_loop(0, SEQ // MERGE_ROWS, emit, 0)


def _attn(zp):
    width = ATT_HEADS * HEAD_DIM
    specs = []
    for g in range(N_GROUPS):
        for t in range(3):
            specs.append(pl.BlockSpec((1, SEQ, width), lambda b, h, s=3 * g + t: (s, b, h)))
    masks = [jnp.asarray(mk) for mk in _attn_masks()]
    specs += [pl.BlockSpec(mk.shape, lambda b, h: (0, 0)) for mk in masks]
    planes = N_GROUPS * ATT_HEADS
    return pl.pallas_call(
        _attn_kernel,
        grid=(BATCH, HEADS_PER_GROUP // ATT_HEADS),
        in_specs=specs,
        out_specs=pl.BlockSpec((SEQ, width), lambda b, h: (b, h)),
        out_shape=jax.ShapeDtypeStruct((M_TOKENS, ATT_OUT_WIDTH), BF16),
        scratch_shapes=[pltpu.VMEM((planes, SEQ, HEAD_DIM), F32),
                        pltpu.VMEM((planes, SEQ, HEAD_DIM), F32),
                        pltpu.VMEM((ATT_HEADS, SEQ, HEAD_DIM), F32)],
        compiler_params=pltpu.CompilerParams(
            dimension_semantics=("parallel", "parallel"), vmem_limit_bytes=VMEM_LIMIT),
        name="attn",
    )(*([zp] * 9), *masks)


MO_ROWS = 512


def _merge_out_kernel(yr_ref, ya_ref, grnn_ref, gatt_ref, x_ref, pr_ref, pa_ref, wo_ref, ln2_ref,
                      xo_ref, h2_ref):
    t_rnn = jnp.dot(yr_ref[...], pr_ref[...], preferred_element_type=F32)
    t_att = jnp.dot(ya_ref[...], pa_ref[...], preferred_element_type=F32)
    halves = []
    for hf in range(D_MODEL // SLAB):
        cols = slice(hf * SLAB, (hf + 1) * SLAB)
        g_rnn = jax.nn.sigmoid(grnn_ref[hf].astype(F32))
        g_att = jax.nn.sigmoid(gatt_ref[hf].astype(F32))
        halves.append((g_rnn * t_rnn[:, cols] + g_att * t_att[:, cols]).astype(BF16))
    merged = jnp.concatenate(halves, axis=1)
    xn = x_ref[...] + jnp.dot(merged, wo_ref[...], preferred_element_type=F32)
    xo_ref[...] = xn
    h2_ref[...] = (_rms(xn) * ln2_ref[...]).astype(BF16)


def _merge_out(layer, y_rnn, y_att, z, x2, pr_bf, pa_bf, wo_bf, ln2):
    slabs_per_gate = D_MODEL // SLAB
    full = lambda shape: pl.BlockSpec((None,) + shape, lambda i: (layer, 0, 0))
    return pl.pallas_call(
        _merge_out_kernel,
        grid=(M_TOKENS // MO_ROWS,),
        in_specs=[
            pl.BlockSpec((MO_ROWS, D_RNN), lambda i: (i, 0)),
            pl.BlockSpec((MO_ROWS, ATT_OUT_WIDTH), lambda i: (i, 0)),
            pl.BlockSpec((slabs_per_gate, MO_ROWS, SLAB), lambda i: (SLAB_GRNN // slabs_per_gate, i, 0)),
            pl.BlockSpec((slabs_per_gate, MO_ROWS, SLAB), lambda i: (SLAB_GATT // slabs_per_gate, i, 0)),
            pl.BlockSpec((MO_ROWS, D_MODEL), lambda i: (i, 0)),
            full((D_RNN, D_MODEL)), full((ATT_OUT_WIDTH, D_MODEL)), full((D_MODEL, D_MODEL)),
            full((1, D_MODEL)),
        ],
        out_specs=[pl.BlockSpec((MO_ROWS, D_MODEL), lambda i: (i, 0)),
                   pl.BlockSpec((MO_ROWS, D_MODEL), lambda i: (i, 0))],
        out_shape=[jax.ShapeDtypeStruct((M_TOKENS, D_MODEL), F32),
                   jax.ShapeDtypeStruct((M_TOKENS, D_MODEL), BF16)],
        compiler_params=pltpu.CompilerParams(
            dimension_semantics=("parallel",), vmem_limit_bytes=VMEM_LIMIT),
        name="merge_out",
    )(y_rnn, y_att, z, z, x2, pr_bf, pa_bf, wo_bf, ln2)


FFN_ROWS = 1024
FFN_COLS = 512
FFN_CHUNK = 512
FFN_HALO = BF16_ROWS


def _ffn_kernel(emit_h, h2_ref, halo_ref, wup_ref, cw_ref, cb_ref, wd_ref, x_ref, *rest):
    if emit_h:
        ln_ref, o_ref, h_ref, hbuf = rest
    else:
        o_ref, hbuf = rest
    seq_start = pl.program_id(0) % (SEQ // FFN_ROWS) == 0

    @pl.when(seq_start)
    def _():
        hbuf[0:FFN_HALO, :] = jnp.zeros((FFN_HALO, D_MODEL), BF16)

    @pl.when(jnp.logical_not(seq_start))
    def _():
        hbuf[0:FFN_HALO, :] = halo_ref[...]

    hbuf[FFN_HALO:, :] = h2_ref[...]
    n_k = D_FF // FFN_COLS

    def up(hb, k):
        cols = [slice(base + k * FFN_COLS, base + (k + 1) * FFN_COLS) for base in (0, D_FF)]
        return [(jnp.dot(hb, wup_ref[:, c], preferred_element_type=F32), c) for c in cols]

    def conv(u, cols):
        out = cb_ref[:, cols]
        for t in range(FFN_CONV_WIDTH):
            back = FFN_CONV_WIDTH - 1 - t
            tap = u if back == 0 else pltpu.roll(u, back, axis=0)
            out = out + tap[FFN_HALO:] * cw_ref[t:t + 1, cols]
        return out

    def chunk(c, carry):
        r0 = pl.multiple_of(c * FFN_CHUNK, FFN_CHUNK)
        hb = hbuf[pl.ds(r0, FFN_CHUNK + FFN_HALO), :]
        acc = x_ref[pl.ds(r0, FFN_CHUNK), :]
        pre = up(hb, 0)
        for k in range(n_k):
            nxt = up(hb, k + 1) if k + 1 < n_k else None
            (ug, gcols), (uv, vcols) = pre
            act = (jax.nn.gelu(conv(ug, gcols)) * conv(uv, vcols)).astype(BF16)
            acc = acc + jnp.dot(act, wd_ref[k * FFN_COLS:(k + 1) * FFN_COLS, :], preferred_element_type=F32)
            pre = nxt
        o_ref[pl.ds(r0, FFN_CHUNK), :] = acc
        if emit_h:
            h_ref[pl.ds(r0, FFN_CHUNK), :] = (_rms(acc) * ln_ref[...]).astype(BF16)
        return carry

    lax.fori_loop(0, FFN_ROWS // FFN_CHUNK, chunk, 0)


def _ffn(layer, h2, x2, wup_bf, conv_w, conv_b, wd_bf, ln1):
    emit_h = layer + 1 < DEPTH
    halo_blocks = FFN_ROWS // FFN_HALO
    resident = lambda shape: pl.BlockSpec((None,) + shape, lambda i: (layer, 0, 0),
                                          pipeline_mode=pl.Buffered(1))
    tile = pl.BlockSpec((FFN_ROWS, D_MODEL), lambda i: (i, 0))
    in_specs = [
        tile,
        pl.BlockSpec((FFN_HALO, D_MODEL), lambda i: (jnp.maximum(i * halo_blocks - 1, 0), 0)),
        resident((D_MODEL, 2 * D_FF)),
        resident((FFN_CONV_WIDTH, 2 * D_FF)),
        resident((1, 2 * D_FF)),
        resident((D_FF, D_MODEL)),
        tile,
    ]
    args = [h2, h2, wup_bf, conv_w, conv_b, wd_bf, x2]
    out_specs = [tile]
    out_shape = [jax.ShapeDtypeStruct((M_TOKENS, D_MODEL), F32)]
    if emit_h:
        in_specs.append(pl.BlockSpec((None, 1, D_MODEL), lambda i: (layer + 1, 0, 0)))
        args.append(ln1)
        out_specs.append(tile)
        out_shape.append(jax.ShapeDtypeStruct((M_TOKENS, D_MODEL), BF16))
    outs = pl.pallas_call(
        functools.partial(_ffn_kernel, emit_h),
        grid=(M_TOKENS // FFN_ROWS,),
        in_specs=in_specs,
        out_specs=out_specs,
        out_shape=out_shape,
        scratch_shapes=[pltpu.VMEM((FFN_ROWS + FFN_HALO, D_MODEL), BF16)],
        compiler_params=pltpu.CompilerParams(
            dimension_semantics=("parallel",), vmem_limit_bytes=VMEM_LIMIT),
        name="ffn",
    )(*args)
    return (outs[0], outs[1]) if emit_h else (outs[0], None)


def _norm_kernel(x_ref, ln_ref, h_ref):
    h_ref[...] = (_rms(x_ref[...]) * ln_ref[...]).astype(BF16)


def _first_norm(x2, ln1):
    tile = pl.BlockSpec((MO_ROWS, D_MODEL), lambda i: (i, 0))
    return pl.pallas_call(
        _norm_kernel,
        grid=(M_TOKENS // MO_ROWS,),
        in_specs=[tile, pl.BlockSpec((None, 1, D_MODEL), lambda i: (0, 0, 0))],
        out_specs=tile,
        out_shape=jax.ShapeDtypeStruct((M_TOKENS, D_MODEL), BF16),
        compiler_params=pltpu.CompilerParams(dimension_semantics=("parallel",)),
        name="first_norm",
    )(x2, ln1)


def kernel(x, positions, ln1_g, w_in, rnn_conv_w, rnn_conv_b, lru_wa, lru_ba, lru_wx, lru_bx, lru_lambda, q_norm_g, k_norm_g, proj_rnn, proj_attn, w_out, ln2_g, w_up, ffn_conv_w, ffn_conv_b, w_down):
    x2 = x.reshape(M_TOKENS, D_MODEL)
    half = HEAD_DIM // 2
    inv_freq = ROPE_THETA ** (-jnp.arange(half, dtype=F32) / half)
    inv2 = jnp.concatenate([inv_freq, inv_freq]).reshape(1, HEAD_DIM)
    cos_t, sin_t = _rope_tables(positions.reshape(M_TOKENS, 1), inv2)

    rows = lambda v: v.reshape(DEPTH, 1, -1)
    w_in, lru_wa, lru_wx, proj_rnn, proj_attn, w_out, w_up, w_down = (
        w.astype(BF16) for w in (w_in, lru_wa, lru_wx, proj_rnn, proj_attn, w_out, w_up, w_down))
    ln1, ln2, q_gain, k_gain = rows(ln1_g), rows(ln2_g), rows(q_norm_g), rows(k_norm_g)
    conv_b, ba, bx, lam, ffn_b = (rows(v) for v in (rnn_conv_b, lru_ba, lru_bx, lru_lambda, ffn_conv_b))

    h = _first_norm(x2, ln1)
    for l in range(DEPTH):
        zn, zp = _in_proj(l, h, w_in, cos_t, sin_t, q_gain, k_gain)
        y_rnn = _rnn(l, zn, rnn_conv_w, conv_b, lru_wa, ba, lru_wx, bx, lam)
        y_att = _attn(zp)
        x2, h2 = _merge_out(l, y_rnn, y_att, zn, x2, proj_rnn, proj_attn, w_out, ln2)
        x2, h = _ffn(l, h2, x2, w_up, ffn_conv_w, ffn_b, w_down, ln1)
    return x2.reshape(BATCH, SEQ, D_MODEL)
```
